```python
import math
import jax, jax.numpy as jnp
from jax import lax
import numpy as np

D_MODEL = 1024
BATCH = 8
SEQ = 4096
DEPTH = 1

FOURIER_GROUPS = 8
FOURIER_GROUP_DIM = 128
FOURIER_WIDTH = FOURIER_GROUPS * FOURIER_GROUP_DIM
ATTN_HEAD_DIM = 64
N_HEADS = D_MODEL // (2 * ATTN_HEAD_DIM)
QK_WIDTH = N_HEADS * 2 * ATTN_HEAD_DIM
V_WIDTH = N_HEADS * 2 * ATTN_HEAD_DIM
IN_WIDTH = FOURIER_WIDTH + 2 * QK_WIDTH + V_WIDTH
N_BRANCHES = 2
D_FF = ((-(-8 * D_MODEL // 3) + 255) // 256) * 256
PLE_DIM = 256
Q_BLOCK = 128
RMS_EPS = 1e-6

kernel_name = "hybrid_fnet_diffattn_encoder_block"


def rmsnorm(x, g):
    xf = x.astype(jnp.float32)
    y = xf * lax.rsqrt(jnp.mean(xf * xf, axis=-1, keepdims=True) + RMS_EPS)
    return (y * g.astype(jnp.float32)).astype(x.dtype)


def alibi_slopes(n_heads):
    return 2.0 ** (-8.0 * jnp.arange(1, n_heads + 1, dtype=jnp.float32) / n_heads)


def lambda_init_fn(layer_idx):
    return 0.8 - 0.6 * math.exp(-0.3 * layer_idx)


def fourier_mix(u):
    b, s, _ = u.shape
    ug = u.reshape(b, s, FOURIER_GROUPS, FOURIER_GROUP_DIM).astype(jnp.float32)
    f = jnp.fft.fftn(ug, axes=(1, 3), norm="ortho").real
    return f.reshape(b, s, FOURIER_WIDTH).astype(u.dtype)


def diff_attention(q, k, v, lam, slopes, subln_g, lambda_init):
    b, s = q.shape[0], q.shape[1]
    nblk = s // Q_BLOCK
    qb = q.reshape(b, nblk, Q_BLOCK, N_HEADS, 2, ATTN_HEAD_DIM).transpose(1, 0, 2, 3, 4, 5)
    kpos = jnp.arange(s, dtype=jnp.float32)
    scale = ATTN_HEAD_DIM ** -0.5

    def block(args):
        qi, i = args
        qpos = (i * Q_BLOCK).astype(jnp.float32) + jnp.arange(Q_BLOCK, dtype=jnp.float32)
        bias = -slopes[:, None, None] * jnp.abs(qpos[:, None] - kpos[None, :])[None]
        sc = jnp.einsum('bqhcd,bkhcd->bhcqk', qi, k).astype(jnp.float32) * scale
        a = jax.nn.softmax(sc + bias[None, :, None], axis=-1)
        w = a[:, :, 0] - lam.astype(jnp.float32) * a[:, :, 1]
        return jnp.einsum('bhqk,bkhe->bqhe', w.astype(v.dtype), v)

    o = lax.map(block, (qb, jnp.arange(nblk, dtype=jnp.int32)))
    o = o.transpose(1, 0, 2, 3, 4).reshape(b, s, N_HEADS, 2 * ATTN_HEAD_DIM)
    o = rmsnorm(o, subln_g) * (1.0 - lambda_init)
    return o.reshape(b, s, V_WIDTH)


def setup_inputs(seed: int = 0) -> dict:
    key = jax.random.key(seed)
    ks = jax.random.split(key, 24)
    f32 = jnp.float32

    def w(k, shape, fan_in):
        return jax.random.normal(k, shape, f32) * (fan_in ** -0.5)

    def gain(k, shape):
        return 1.0 + 0.02 * jax.random.normal(k, shape, f32)

    L = DEPTH
    return {
        "x": jax.random.normal(ks[0], (BATCH, SEQ, D_MODEL), f32),
        "p": jax.random.normal(ks[1], (DEPTH, BATCH, SEQ, PLE_DIM), f32),
        "g_mix": gain(ks[2], (L, D_MODEL)),
        "w_in": w(ks[3], (L, D_MODEL, IN_WIDTH), D_MODEL),
        "lambda_q1": 0.1 * jax.random.normal(ks[4], (L, ATTN_HEAD_DIM), f32),
        "lambda_k1": 0.1 * jax.random.normal(ks[5], (L, ATTN_HEAD_DIM), f32),
        "lambda_q2": 0.1 * jax.random.normal(ks[6], (L, ATTN_HEAD_DIM), f32),
        "lambda_k2": 0.1 * jax.random.normal(ks[7], (L, ATTN_HEAD_DIM), f32),
        "g_subln": gain(ks[8], (L, 2 * ATTN_HEAD_DIM)),
        "w_fourier_out": w(ks[9], (L, FOURIER_WIDTH, D_MODEL), FOURIER_WIDTH),
        "w_attn_out": w(ks[10], (L, V_WIDTH, D_MODEL), V_WIDTH),
        "w_branch_gate": w(ks[11], (L, D_MODEL, N_BRANCHES * D_MODEL), D_MODEL),
        "b_branch_gate": 0.02 * jax.random.normal(ks[12], (L, N_BRANCHES * D_MODEL), f32),
        "w_o": w(ks[13], (L, D_MODEL, D_MODEL), D_MODEL),
        "g_ffn": gain(ks[14], (L, D_MODEL)),
        "w_ffn_gate": w(ks[15], (L, D_MODEL, D_FF), D_MODEL),
        "w_ffn_up": w(ks[16], (L, D_MODEL, D_FF), D_MODEL),
        "w_ffn_down": w(ks[17], (L, D_FF, D_MODEL), D_FF),
        "g_ple": gain(ks[18], (L, D_MODEL)),
        "w_ple_gate": w(ks[19], (L, D_MODEL, D_MODEL), D_MODEL),
        "w_ple_proj": w(ks[20], (L, PLE_DIM, D_MODEL), PLE_DIM),
        "g_final": gain(ks[21], (D_MODEL,)),
    }


def reference(x, p, g_mix, w_in, lambda_q1, lambda_k1, lambda_q2, lambda_k2, g_subln,
              w_fourier_out, w_attn_out, w_branch_gate, b_branch_gate, w_o,
              g_ffn, w_ffn_gate, w_ffn_up, w_ffn_down, g_ple, w_ple_gate, w_ple_proj,
              g_final):
    b, s, d = x.shape
    slopes = alibi_slopes(N_HEADS)
    for i in range(DEPTH):
        lam_init = lambda_init_fn(i)
        h = rmsnorm(x, g_mix[i])
        u = h @ w_in[i]
        u_f = u[..., :FOURIER_WIDTH]
        q = u[..., FOURIER_WIDTH:FOURIER_WIDTH + QK_WIDTH].reshape(b, s, N_HEADS, 2, ATTN_HEAD_DIM)
        k = u[..., FOURIER_WIDTH + QK_WIDTH:FOURIER_WIDTH + 2 * QK_WIDTH].reshape(b, s, N_HEADS, 2, ATTN_HEAD_DIM)
        v = u[..., FOURIER_WIDTH + 2 * QK_WIDTH:].reshape(b, s, N_HEADS, 2 * ATTN_HEAD_DIM)

        y_f = fourier_mix(u_f) @ w_fourier_out[i]

        lam = (jnp.exp(jnp.sum(lambda_q1[i].astype(jnp.float32) * lambda_k1[i].astype(jnp.float32)))
               - jnp.exp(jnp.sum(lambda_q2[i].astype(jnp.float32) * lambda_k2[i].astype(jnp.float32)))
               + lam_init)
        y_a = diff_attention(q, k, v, lam, slopes, g_subln[i], lam_init) @ w_attn_out[i]

        gates = jax.nn.sigmoid(h @ w_branch_gate[i] + b_branch_gate[i]).reshape(b, s, N_BRANCHES, d)
        merged = gates[:, :, 0] * y_f + gates[:, :, 1] * y_a
        x = x + merged @ w_o[i]

        h2 = rmsnorm(x, g_ffn[i])
        x = x + (jax.nn.silu(h2 @ w_ffn_gate[i]) * (h2 @ w_ffn_up[i])) @ w_ffn_down[i]

        h3 = rmsnorm(x, g_ple[i])
        x = x + jax.nn.sigmoid(h3 @ w_ple_gate[i]) * (p[i] @ w_ple_proj[i])
    return rmsnorm(x, g_final)
```

```python
import functools
import math

import jax
import jax.numpy as jnp
from jax import lax
from jax.experimental import pallas as pl
from jax.experimental.pallas import tpu as pltpu

F32 = jnp.float32
BF16 = jnp.bfloat16

N_HEADS = 8
HEAD_DIM = 64
HEAD_W = 2 * HEAD_DIM
GROUP_DIM = 128
RMS_EPS = 1e-6
LAMBDA_INIT = 0.8 - 0.6 * math.exp(-0.3 * 0)
V_ROWS = HEAD_W + 16
VMEM_LIMIT = 56 * 1024 * 1024


def _rms(x, g):
    return x * lax.rsqrt(jnp.mean(x * x, axis=-1, keepdims=True) + RMS_EPS) * g


def _dot(a, b):
    return jnp.dot(a, b, preferred_element_type=F32)


def _in_proj_kernel(x_ref, g_ref, w_ref, cs_ref, a_ref, b_ref, qT_ref, k_ref, vT_ref):
    d = x_ref.shape[1]
    tm = x_ref.shape[0]
    h = _rms(x_ref[...], g_ref[...]).astype(BF16)
    uf = _dot(h, w_ref[:, 0:d])
    cs = cs_ref[...]
    for g in range(d // GROUP_DIM):
        lo, hi = g * GROUP_DIM, (g + 1) * GROUP_DIM
        ab = _dot(uf[:, lo:hi].astype(BF16), cs)
        a_ref[:, lo:hi] = ab[:, :GROUP_DIM].astype(BF16)
        b_ref[:, lo:hi] = ab[:, GROUP_DIM:].astype(BF16)
    q = _dot(h, w_ref[:, d:2 * d]) * (HEAD_DIM ** -0.5)
    qT_ref[...] = q.T.astype(BF16)
    k_ref[...] = _dot(h, w_ref[:, 2 * d:3 * d]).astype(BF16)
    vT = _dot(h, w_ref[:, 3 * d:4 * d]).T
    ones = jnp.ones((V_ROWS - HEAD_W, tm), BF16)
    for hh in range(N_HEADS):
        vT_ref[hh * V_ROWS:hh * V_ROWS + HEAD_W, :] = vT[hh * HEAD_W:(hh + 1) * HEAD_W, :].astype(BF16)
        vT_ref[hh * V_ROWS + HEAD_W:(hh + 1) * V_ROWS, :] = ones


def _fourier_kernel(fc_ref, fs_ref, a_ref, b_ref, wfo_ref, y_ref):
    y = _dot(fc_ref[...], a_ref[...]) + _dot(fs_ref[...], b_ref[...])
    y_ref[...] = _dot(y.astype(BF16), wfo_ref[...])


def _attn_kernel(slopes_ref, qT_ref, k_ref, vT_ref, lq1_ref, lk1_ref, lq2_ref, lk2_ref,
                 gs_ref, o_ref, *, tk):
    hd = pl.program_id(1)
    t = pl.program_id(2)
    tq = qT_ref.shape[1]
    s_len = k_ref.shape[0]
    slope = slopes_ref[hd]

    qT = qT_ref[...]
    zero = jnp.zeros((HEAD_DIM, tq), BF16)
    qa = jnp.concatenate([qT[:HEAD_DIM], zero], axis=0)
    qb = jnp.concatenate([zero, qT[HEAD_DIM:]], axis=0)
    qpos = (t * tq + lax.broadcasted_iota(jnp.int32, (1, tq), 1)).astype(F32)
    kio = lax.broadcasted_iota(jnp.int32, (tk, 1), 0).astype(F32)

    def body(c, carry):
        m1, m2, acc1, acc2 = carry
        k0 = pl.multiple_of(c * tk, tk)
        kc = k_ref[pl.ds(k0, tk), :]
        vc = vT_ref[:, pl.ds(k0, tk)]
        kpos = kio + (c * tk).astype(F32)
        bias = -slope * jnp.abs(kpos - qpos)
        s1 = _dot(kc, qa) + bias
        s2 = _dot(kc, qb) + bias
        m1n = jnp.maximum(m1, jnp.max(s1, axis=0, keepdims=True))
        m2n = jnp.maximum(m2, jnp.max(s2, axis=0, keepdims=True))
        p1 = jnp.exp(s1 - m1n).astype(BF16)
        p2 = jnp.exp(s2 - m2n).astype(BF16)
        acc1 = jnp.exp(m1 - m1n) * acc1 + _dot(vc, p1)
        acc2 = jnp.exp(m2 - m2n) * acc2 + _dot(vc, p2)
        return m1n, m2n, acc1, acc2

    m0 = jnp.full((1, tq), -1e30, F32)
    a0 = jnp.zeros((V_ROWS, tq), F32)
    _, _, acc1, acc2 = lax.fori_loop(0, s_len // tk, body, (m0, m0, a0, a0))

    lam = (jnp.exp(jnp.sum(lq1_ref[...] * lk1_ref[...], axis=1, keepdims=True))
           - jnp.exp(jnp.sum(lq2_ref[...] * lk2_ref[...], axis=1, keepdims=True))
           + LAMBDA_INIT)
    o1 = acc1[:HEAD_W] / acc1[HEAD_W:HEAD_W + 1]
    o2 = acc2[:HEAD_W] / acc2[HEAD_W:HEAD_W + 1]
    o = o1 - lam * o2
    on = o * lax.rsqrt(jnp.mean(o * o, axis=0, keepdims=True) + RMS_EPS)
    o_ref[...] = (on.T * (gs_ref[...] * (1.0 - LAMBDA_INIT))).astype(BF16)


def _tail_kernel(x_ref, yf_ref, ya_ref, p_ref, gmix_ref, wbg_ref, bbg_ref, wao_ref, wo_ref,
                 gffn_ref, wg_ref, wu_ref, wd_ref, gple_ref, wpg_ref, wpp_ref, gfin_ref, out_ref):
    d = x_ref.shape[1]
    x = x_ref[...]
    h = _rms(x, gmix_ref[...]).astype(BF16)
    gates = jax.nn.sigmoid(_dot(h, wbg_ref[...]) + bbg_ref[...])
    ya = _dot(ya_ref[...], wao_ref[...])
    merged = gates[:, :d] * yf_ref[...] + gates[:, d:] * ya
    x = x + _dot(merged.astype(BF16), wo_ref[...])

    h2 = _rms(x, gffn_ref[...]).astype(BF16)
    gg = _dot(h2, wg_ref[...])
    uu = _dot(h2, wu_ref[...])
    act = (gg * jax.nn.sigmoid(gg)) * uu
    x = x + _dot(act.astype(BF16), wd_ref[...])

    h3 = _rms(x, gple_ref[...]).astype(BF16)
    pg = jax.nn.sigmoid(_dot(h3, wpg_ref[...]))
    x = x + pg * _dot(p_ref[...].astype(BF16), wpp_ref[...])
    out_ref[...] = _rms(x, gfin_ref[...])


def _const_spec(shape):
    return pl.BlockSpec(shape, lambda *_: (0,) * len(shape), pipeline_mode=pl.Buffered(1))


def _dft_tables(s_len):
    n = jnp.arange(s_len, dtype=jnp.int32)
    r = (n[:, None] * n[None, :]) % s_len
    ang = r.astype(F32) * (2.0 * math.pi / s_len)
    sc = s_len ** -0.5
    fc = (jnp.cos(ang) * sc).astype(BF16)
    fs = (jnp.sin(ang) * (-sc)).astype(BF16)
    c = jnp.arange(GROUP_DIM, dtype=jnp.int32)
    rc = (c[:, None] * c[None, :]) % GROUP_DIM
    angc = rc.astype(F32) * (2.0 * math.pi / GROUP_DIM)
    gc = GROUP_DIM ** -0.5
    cs = jnp.concatenate([jnp.cos(angc) * gc, jnp.sin(angc) * gc], axis=1).astype(BF16)
    return fc, fs, cs


def kernel(x, p, g_mix, w_in, lambda_q1, lambda_k1, lambda_q2, lambda_k2, g_subln,
           w_fourier_out, w_attn_out, w_branch_gate, b_branch_gate, w_o, g_ffn, w_ffn_gate,
           w_ffn_up, w_ffn_down, g_ple, w_ple_gate, w_ple_proj, g_final):
    bsz, s_len, d = x.shape
    t_tok = bsz * s_len
    d_ff = w_ffn_gate.shape[-1]
    ple = p.shape[-1]
    assert w_in.shape[0] == 1 and d == N_HEADS * HEAD_W
    x2 = x.reshape(t_tok, d)
    p2 = p[0].reshape(t_tok, ple)
    fc, fs, cs = _dft_tables(s_len)
    cparams = functools.partial(pltpu.CompilerParams, vmem_limit_bytes=VMEM_LIMIT)

    tm = min(512, s_len)
    row = lambda i: (i, 0)
    col = lambda i: (0, i)
    a_arr, b_arr, qT, k_arr, vT = pl.pallas_call(
        _in_proj_kernel,
        grid=(t_tok // tm,),
        in_specs=[pl.BlockSpec((tm, d), row), _const_spec((1, d)),
                  _const_spec((d, 4 * d)), _const_spec((GROUP_DIM, 2 * GROUP_DIM))],
        out_specs=[pl.BlockSpec((tm, d), row), pl.BlockSpec((tm, d), row),
                   pl.BlockSpec((d, tm), col), pl.BlockSpec((tm, d), row),
                   pl.BlockSpec((N_HEADS * V_ROWS, tm), col)],
        out_shape=[jax.ShapeDtypeStruct((t_tok, d), BF16), jax.ShapeDtypeStruct((t_tok, d), BF16),
                   jax.ShapeDtypeStruct((d, t_tok), BF16), jax.ShapeDtypeStruct((t_tok, d), BF16),
                   jax.ShapeDtypeStruct((N_HEADS * V_ROWS, t_tok), BF16)],
        compiler_params=cparams(dimension_semantics=("arbitrary",)),
        name="in_proj",
    )(x2, g_mix[0][None, :], w_in[0].astype(BF16), cs)

    tr = min(256, s_len)
    nr = s_len // tr
    y_f = pl.pallas_call(
        _fourier_kernel,
        grid=(bsz, nr),
        in_specs=[pl.BlockSpec((tr, s_len), lambda b, i: (i, 0)),
                  pl.BlockSpec((tr, s_len), lambda b, i: (i, 0)),
                  pl.BlockSpec((s_len, d), lambda b, i: (b, 0)),
                  pl.BlockSpec((s_len, d), lambda b, i: (b, 0)),
                  _const_spec((d, d))],
        out_specs=pl.BlockSpec((tr, d), lambda b, i: (b * nr + i, 0)),
        out_shape=jax.ShapeDtypeStruct((t_tok, d), F32),
        compiler_params=cparams(dimension_semantics=("arbitrary", "arbitrary")),
        name="fourier",
    )(fc, fs, a_arr, b_arr, w_fourier_out[0].astype(BF16))

    tq = min(256, s_len)
    tk = min(512, s_len)
    nq = s_len // tq
    slopes = 2.0 ** (-8.0 * jnp.arange(1, N_HEADS + 1, dtype=F32) / N_HEADS)
    lam_spec = pl.BlockSpec((1, HEAD_DIM), lambda b, h, t: (0, 0))
    y_a = pl.pallas_call(
        functools.partial(_attn_kernel, tk=tk),
        grid=(bsz, N_HEADS, nq),
        in_specs=[pl.BlockSpec(memory_space=pltpu.SMEM),
                  pl.BlockSpec((HEAD_W, tq), lambda b, h, t: (h, b * nq + t)),
                  pl.BlockSpec((s_len, HEAD_W), lambda b, h, t: (b, h)),
                  pl.BlockSpec((V_ROWS, s_len), lambda b, h, t: (h, b)),
                  lam_spec, lam_spec, lam_spec, lam_spec,
                  pl.BlockSpec((1, HEAD_W), lambda b, h, t: (0, 0))],
        out_specs=pl.BlockSpec((tq, HEAD_W), lambda b, h, t: (b * nq + t, h)),
        out_shape=jax.ShapeDtypeStruct((t_tok, d), BF16),
        compiler_params=cparams(dimension_semantics=("arbitrary", "arbitrary", "arbitrary")),
        name="diff_attn",
    )(slopes, qT, k_arr, vT, lambda_q1, lambda_k1, lambda_q2, lambda_k2, g_subln)

    tt = min(256, s_len)
    vec = lambda a: a.reshape(1, -1)
    out = pl.pallas_call(
        _tail_kernel,
        grid=(t_tok // tt,),
        in_specs=[pl.BlockSpec((tt, d), row), pl.BlockSpec((tt, d), row),
                  pl.BlockSpec((tt, d), row), pl.BlockSpec((tt, ple), row),
                  _const_spec((1, d)), _const_spec((d, 2 * d)), _const_spec((1, 2 * d)),
                  _const_spec((d, d)), _const_spec((d, d)),
                  _const_spec((1, d)), _const_spec((d, d_ff)), _const_spec((d, d_ff)),
                  _const_spec((d_ff, d)),
                  _const_spec((1, d)), _const_spec((d, d)), _const_spec((ple, d)),
                  _const_spec((1, d))],
        out_specs=pl.BlockSpec((tt, d), row),
        out_shape=jax.ShapeDtypeStruct((t_tok, d), F32),
        compiler_params=cparams(dimension_semantics=("arbitrary",)),
        name="tail",
    )(x2, y_f, y_a, p2, vec(g_mix[0]), w_branch_gate[0].astype(BF16), vec(b_branch_gate[0]),
      w_attn_out[0].astype(BF16), w_o[0].astype(BF16), vec(g_ffn[0]),
      w_ffn_gate[0].astype(BF16), w_ffn_up[0].astype(BF16), w_ffn_down[0].astype(BF16),
      vec(g_ple[0]), w_ple_gate[0].astype(BF16), w_ple_proj[0].astype(BF16), vec(g_final))
    return out.reshape(bsz, s_len, d)
```

```python
import functools
import math

import jax
import jax.numpy as jnp
import numpy as np
from jax import lax
from jax.experimental import pallas as pl
from jax.experimental.pallas import tpu as pltpu

F32 = jnp.float32
BF16 = jnp.bfloat16

N_HEADS = 8
HEAD_DIM = 64
HEAD_W = 2 * HEAD_DIM
GROUP_DIM = 128
RMS_EPS = 1e-6
LAMBDA_INIT = 0.8 - 0.6 * math.exp(-0.3 * 0)
V_ROWS = HEAD_W + 16
VMEM_LIMIT = 56 * 1024 * 1024
Q_SCALE = HEAD_DIM ** -0.5 * math.log2(math.e)


def _rms(x, g):
    return x * lax.rsqrt(jnp.mean(x * x, axis=-1, keepdims=True) + RMS_EPS) * g


def _dot(a, b):
    return jnp.dot(a, b, preferred_element_type=F32)


def _in_proj_kernel(x_ref, g_ref, w_ref, cs_ref, a_ref, b_ref, qT_ref, k_ref, vT_ref):
    d = x_ref.shape[1]
    tm = x_ref.shape[0]
    h = _rms(x_ref[...], g_ref[...]).astype(BF16)
    uf = _dot(h, w_ref[:, 0:d])
    cs = cs_ref[...]
    for g in range(d // GROUP_DIM):
        lo, hi = g * GROUP_DIM, (g + 1) * GROUP_DIM
        ab = _dot(uf[:, lo:hi].astype(BF16), cs)
        a_ref[:, lo:hi] = ab[:, :GROUP_DIM].astype(BF16)
        b_ref[:, lo:hi] = ab[:, GROUP_DIM:].astype(BF16)
    q = _dot(h, w_ref[:, d:2 * d]) * Q_SCALE
    qT_ref[...] = q.T.astype(BF16)
    k_ref[...] = _dot(h, w_ref[:, 2 * d:3 * d]).astype(BF16)
    vT = _dot(h, w_ref[:, 3 * d:4 * d]).T
    ones = jnp.ones((V_ROWS - HEAD_W, tm), BF16)
    for hh in range(N_HEADS):
        vT_ref[hh * V_ROWS:hh * V_ROWS + HEAD_W, :] = vT[hh * HEAD_W:(hh + 1) * HEAD_W, :].astype(BF16)
        vT_ref[hh * V_ROWS + HEAD_W:(hh + 1) * V_ROWS, :] = ones


def _fourier_kernel(fc_ref, fs_ref, a_ref, b_ref, wfo_ref, y_ref):
    y = _dot(fc_ref[...], a_ref[...]) + _dot(fs_ref[...], b_ref[...])
    y_ref[...] = _dot(y.astype(BF16), wfo_ref[...])


def _attn_kernel(cvec_ref, qT_ref, k_ref, vT_ref, augk_ref, augq_ref, lq1_ref, lk1_ref, lq2_ref,
                 lk2_ref, gs_ref, o_ref, w_ref, s1_ref, s2_ref, *, tk):
    hd = pl.program_id(1)
    t = pl.program_id(2)
    tq = qT_ref.shape[1]
    s_len = k_ref.shape[0]
    nk = s_len // tk
    c_h = cvec_ref[hd]
    j0 = t * tq
    cd = j0 // tk

    qT = qT_ref[...]
    zq = jnp.zeros((HEAD_DIM, tq), BF16)
    qa = jnp.concatenate([qT[:HEAD_DIM], zq], axis=0)
    qb = jnp.concatenate([zq, qT[HEAD_DIM:]], axis=0)
    augq = augq_ref[0]
    naugq = -augq
    w_ref[0] = jnp.concatenate([qa, augq], axis=0)
    w_ref[1] = jnp.concatenate([qa, naugq], axis=0)
    w_ref[2] = jnp.concatenate([qb, augq], axis=0)
    w_ref[3] = jnp.concatenate([qb, naugq], axis=0)
    augk = augk_ref[0]

    def chunk_max(s):
        return jnp.max(s.reshape(tk // 8, 8, tq), axis=0)

    m1 = jnp.full((8, tq), -1e30, F32)
    m2 = m1
    for ci in range(nk - 1):
        c = ci + (ci >= cd).astype(jnp.int32)
        after = (c > cd).astype(jnp.int32)
        k0 = pl.multiple_of(c * tk, tk)
        lhs = jnp.concatenate([k_ref[pl.ds(k0, tk), :], augk], axis=1)
        s1 = _dot(lhs, w_ref[after])
        s2 = _dot(lhs, w_ref[2 + after])
        s1_ref[pl.ds(k0, tk), :] = s1
        s2_ref[pl.ds(k0, tk), :] = s2
        off = c_h * ((k0 - j0) * (1 - 2 * after)).astype(F32)
        m1 = jnp.maximum(m1, chunk_max(s1) + off)
        m2 = jnp.maximum(m2, chunk_max(s2) + off)
    kd0 = pl.multiple_of(cd * tk, tk)
    kc = k_ref[pl.ds(kd0, tk), :]
    dpos = ((kd0 - j0) + lax.broadcasted_iota(jnp.int32, (tk, 1), 0)
            - lax.broadcasted_iota(jnp.int32, (1, tq), 1)).astype(F32)
    bias = -c_h * jnp.abs(dpos)
    s1 = _dot(kc, qa) + bias
    s2 = _dot(kc, qb) + bias
    s1_ref[pl.ds(kd0, tk), :] = s1
    s2_ref[pl.ds(kd0, tk), :] = s2
    m1 = jnp.max(jnp.maximum(m1, chunk_max(s1)), axis=0, keepdims=True)
    m2 = jnp.max(jnp.maximum(m2, chunk_max(s2)), axis=0, keepdims=True)

    acc1 = jnp.zeros((V_ROWS, tq), F32)
    acc2 = acc1
    for c in range(nk):
        side = jnp.where(c < cd, 1, jnp.where(c > cd, -1, 0))
        off = c_h * (side * (c * tk - j0)).astype(F32)
        vc = vT_ref[:, c * tk:(c + 1) * tk]
        p1 = jnp.exp2(s1_ref[c * tk:(c + 1) * tk, :] - (m1 - off)).astype(BF16)
        p2 = jnp.exp2(s2_ref[c * tk:(c + 1) * tk, :] - (m2 - off)).astype(BF16)
        acc1 = acc1 + _dot(vc, p1)
        acc2 = acc2 + _dot(vc, p2)

    lam = (jnp.exp(jnp.sum(lq1_ref[...] * lk1_ref[...], axis=1, keepdims=True))
           - jnp.exp(jnp.sum(lq2_ref[...] * lk2_ref[...], axis=1, keepdims=True))
           + LAMBDA_INIT)
    o1 = acc1[:HEAD_W] / acc1[HEAD_W:HEAD_W + 1]
    o2 = acc2[:HEAD_W] / acc2[HEAD_W:HEAD_W + 1]
    o = o1 - lam * o2
    on = o * lax.rsqrt(jnp.mean(o * o, axis=0, keepdims=True) + RMS_EPS)
    o_ref[...] = (on.T * (gs_ref[...] * (1.0 - LAMBDA_INIT))).astype(BF16)


def _alibi_tables(tk, tq):
    c = (2.0 ** (-8.0 * np.arange(1, N_HEADS + 1) / N_HEADS) * math.log2(math.e)).astype(np.float32)

    def split3(x):
        x1 = x.astype(BF16)
        r = x - x1.astype(np.float32)
        x2 = r.astype(BF16)
        x3 = (r - x2.astype(np.float32)).astype(BF16)
        return x1, x2, x3

    augk = np.zeros((N_HEADS, tk, HEAD_W), BF16)
    augq = np.zeros((N_HEADS, HEAD_W, tq), BF16)
    u = c[:, None] * np.arange(tk, dtype=np.float32)[None, :]
    v = c[:, None] * np.arange(tq, dtype=np.float32)[None, :]
    for n, (un, vn) in enumerate(zip(split3(u), split3(v))):
        augk[:, :, n] = un
        augk[:, :, 3 + n] = 1.0
        augq[:, n, :] = 1.0
        augq[:, 3 + n, :] = -vn
    return c, augk, augq


def _tail_kernel(x_ref, yf_ref, ya_ref, p_ref, gmix_ref, wbg_ref, bbg_ref, wao_ref, wo_ref,
                 gffn_ref, wg_ref, wu_ref, wd_ref, gple_ref, wpg_ref, wpp_ref, gfin_ref, out_ref):
    d = x_ref.shape[1]
    x = x_ref[...]
    h = _rms(x, gmix_ref[...]).astype(BF16)
    gates = jax.nn.sigmoid(_dot(h, wbg_ref[...]) + bbg_ref[...])
    ya = _dot(ya_ref[...], wao_ref[...])
    merged = gates[:, :d] * yf_ref[...] + gates[:, d:] * ya
    x = x + _dot(merged.astype(BF16), wo_ref[...])

    h2 = _rms(x, gffn_ref[...]).astype(BF16)
    gg = _dot(h2, wg_ref[...])
    uu = _dot(h2, wu_ref[...])
    act = (gg * jax.nn.sigmoid(gg)) * uu
    x = x + _dot(act.astype(BF16), wd_ref[...])

    h3 = _rms(x, gple_ref[...]).astype(BF16)
    pg = jax.nn.sigmoid(_dot(h3, wpg_ref[...]))
    x = x + pg * _dot(p_ref[...].astype(BF16), wpp_ref[...])
    out_ref[...] = _rms(x, gfin_ref[...])


def _const_spec(shape):
    return pl.BlockSpec(shape, lambda *_: (0,) * len(shape), pipeline_mode=pl.Buffered(1))


def _dft_tables(s_len):
    n = jnp.arange(s_len, dtype=jnp.int32)
    r = (n[:, None] * n[None, :]) % s_len
    ang = r.astype(F32) * (2.0 * math.pi / s_len)
    sc = s_len ** -0.5
    fc = (jnp.cos(ang) * sc).astype(BF16)
    fs = (jnp.sin(ang) * (-sc)).astype(BF16)
    c = jnp.arange(GROUP_DIM, dtype=jnp.int32)
    rc = (c[:, None] * c[None, :]) % GROUP_DIM
    angc = rc.astype(F32) * (2.0 * math.pi / GROUP_DIM)
    gc = GROUP_DIM ** -0.5
    cs = jnp.concatenate([jnp.cos(angc) * gc, jnp.sin(angc) * gc], axis=1).astype(BF16)
    return fc, fs, cs


def kernel(x, p, g_mix, w_in, lambda_q1, lambda_k1, lambda_q2, lambda_k2, g_subln,
           w_fourier_out, w_attn_out, w_branch_gate, b_branch_gate, w_o, g_ffn, w_ffn_gate,
           w_ffn_up, w_ffn_down, g_ple, w_ple_gate, w_ple_proj, g_final):
    bsz, s_len, d = x.shape
    t_tok = bsz * s_len
    d_ff = w_ffn_gate.shape[-1]
    ple = p.shape[-1]
    assert w_in.shape[0] == 1 and d == N_HEADS * HEAD_W
    x2 = x.reshape(t_tok, d)
    p2 = p[0].reshape(t_tok, ple)
    fc, fs, cs = _dft_tables(s_len)
    cparams = functools.partial(pltpu.CompilerParams, vmem_limit_bytes=VMEM_LIMIT)

    tm = min(512, s_len)
    row = lambda i: (i, 0)
    col = lambda i: (0, i)
    a_arr, b_arr, qT, k_arr, vT = pl.pallas_call(
        _in_proj_kernel,
        grid=(t_tok // tm,),
        in_specs=[pl.BlockSpec((tm, d), row), _const_spec((1, d)),
                  _const_spec((d, 4 * d)), _const_spec((GROUP_DIM, 2 * GROUP_DIM))],
        out_specs=[pl.BlockSpec((tm, d), row), pl.BlockSpec((tm, d), row),
                   pl.BlockSpec((d, tm), col), pl.BlockSpec((tm, d), row),
                   pl.BlockSpec((N_HEADS * V_ROWS, tm), col)],
        out_shape=[jax.ShapeDtypeStruct((t_tok, d), BF16), jax.ShapeDtypeStruct((t_tok, d), BF16),
                   jax.ShapeDtypeStruct((d, t_tok), BF16), jax.ShapeDtypeStruct((t_tok, d), BF16),
                   jax.ShapeDtypeStruct((N_HEADS * V_ROWS, t_tok), BF16)],
        compiler_params=cparams(dimension_semantics=("arbitrary",)),
        name="in_proj",
    )(x2, g_mix[0][None, :], w_in[0].astype(BF16), cs)

    tr = min(256, s_len)
    nr = s_len // tr
    y_f = pl.pallas_call(
        _fourier_kernel,
        grid=(bsz, nr),
        in_specs=[pl.BlockSpec((tr, s_len), lambda b, i: (i, 0)),
                  pl.BlockSpec((tr, s_len), lambda b, i: (i, 0)),
                  pl.BlockSpec((s_len, d), lambda b, i: (b, 0)),
                  pl.BlockSpec((s_len, d), lambda b, i: (b, 0)),
                  _const_spec((d, d))],
        out_specs=pl.BlockSpec((tr, d), lambda b, i: (b * nr + i, 0)),
        out_shape=jax.ShapeDtypeStruct((t_tok, d), F32),
        compiler_params=cparams(dimension_semantics=("arbitrary", "arbitrary")),
        name="fourier",
    )(fc, fs, a_arr, b_arr, w_fourier_out[0].astype(BF16))

    tq = min(256, s_len)
    tk = min(512, s_len)
    nq = s_len // tq
    cvec, augk, augq = _alibi_tables(tk, tq)
    lam_spec = pl.BlockSpec((1, HEAD_DIM), lambda b, h, t: (0, 0))
    y_a = pl.pallas_call(
        functools.partial(_attn_kernel, tk=tk),
        grid=(bsz, N_HEADS, nq),
        in_specs=[pl.BlockSpec(memory_space=pltpu.SMEM),
                  pl.BlockSpec((HEAD_W, tq), lambda b, h, t: (h, b * nq + t)),
                  pl.BlockSpec((s_len, HEAD_W), lambda b, h, t: (b, h)),
                  pl.BlockSpec((V_ROWS, s_len), lambda b, h, t: (h, b)),
                  pl.BlockSpec((1, tk, HEAD_W), lambda b, h, t: (h, 0, 0)),
                  pl.BlockSpec((1, HEAD_W, tq), lambda b, h, t: (h, 0, 0)),
                  lam_spec, lam_spec, lam_spec, lam_spec,
                  pl.BlockSpec((1, HEAD_W), lambda b, h, t: (0, 0))],
        out_specs=pl.BlockSpec((tq, HEAD_W), lambda b, h, t: (b * nq + t, h)),
        out_shape=jax.ShapeDtypeStruct((t_tok, d), BF16),
        scratch_shapes=[pltpu.VMEM((4, 2 * HEAD_W, tq), BF16),
                        pltpu.VMEM((s_len, tq), F32), pltpu.VMEM((s_len, tq), F32)],
        compiler_params=cparams(dimension_semantics=("arbitrary", "arbitrary", "arbitrary")),
        name="diff_attn",
    )(jnp.asarray(cvec), qT, k_arr, vT, jnp.asarray(augk), jnp.asarray(augq),
      lambda_q1, lambda_k1, lambda_q2, lambda_k2, g_subln)

    tt = min(256, s_len)
    vec = lambda a: a.reshape(1, -1)
    out = pl.pallas_call(
        _tail_kernel,
        grid=(t_tok // tt,),
        in_specs=[pl.BlockSpec((tt, d), row), pl.BlockSpec((tt, d), row),
                  pl.BlockSpec((tt, d), row), pl.BlockSpec((tt, ple), row),
                  _const_spec((1, d)), _const_spec((d, 2 * d)), _const_spec((1, 2 * d)),
                  _const_spec((d, d)), _const_spec((d, d)),
                  _const_spec((1, d)), _const_spec((d, d_ff)), _const_spec((d, d_ff)),
                  _const_spec((d_ff, d)),
                  _const_spec((1, d)), _const_spec((d, d)), _const_spec((ple, d)),
                  _const_spec((1, d))],
        out_specs=pl.BlockSpec((tt, d), row),
        out_shape=jax.ShapeDtypeStruct((t_tok, d), F32),
        compiler_params=cparams(dimension_semantics=("arbitrary",)),
        name="tail",
    )(x2, y_f, y_a, p2, vec(g_mix[0]), w_branch_gate[0].astype(BF16), vec(b_branch_gate[0]),
      w_attn_out[0].astype(BF16), w_o[0].astype(BF16), vec(g_ffn[0]),
      w_ffn_gate[0].astype(BF16), w_ffn_up[0].astype(BF16), w_ffn_down[0].astype(BF16),
      vec(g_ple[0]), w_ple_gate[0].astype(BF16), w_ple_proj[0].astype(BF16), vec(g_final))
    return out.reshape(bsz, s_len, d)
```

```python
import functools
import math

import jax
import jax.numpy as jnp
import numpy as np
from jax import lax
from jax.experimental import pallas as pl
from jax.experimental.pallas import tpu as pltpu

F32 = jnp.float32
BF16 = jnp.bfloat16

N_HEADS = 8
HEAD_DIM = 64
HEAD_W = 2 * HEAD_DIM
GROUP_DIM = 128
RMS_EPS = 1e-6
LAMBDA_INIT = 0.8 - 0.6 * math.exp(-0.3 * 0)
V_ROWS = HEAD_W + 16
VMEM_LIMIT = 56 * 1024 * 1024
FFT1_COLS = 8
FFT2_GROUP = 8
Q_SCALE = HEAD_DIM ** -0.5 * math.log2(math.e)


def _rms(x, g):
    return x * lax.rsqrt(jnp.mean(x * x, axis=-1, keepdims=True) + RMS_EPS) * g


def _dot(a, b):
    return jnp.dot(a, b, preferred_element_type=F32)


def _in_proj_kernel(x_ref, g_ref, w_ref, uf_ref, qT_ref, k_ref, vT_ref):
    d = x_ref.shape[1]
    tm = x_ref.shape[0]
    h = _rms(x_ref[...], g_ref[...]).astype(BF16)
    uf_ref[...] = _dot(h, w_ref[:, 0:d]).astype(BF16)
    q = _dot(h, w_ref[:, d:2 * d]) * Q_SCALE
    qT_ref[...] = q.T.astype(BF16)
    k_ref[...] = _dot(h, w_ref[:, 2 * d:3 * d]).astype(BF16)
    vT = _dot(h, w_ref[:, 3 * d:4 * d]).T
    ones = jnp.ones((V_ROWS - HEAD_W, tm), BF16)
    for hh in range(N_HEADS):
        vT_ref[hh * V_ROWS:hh * V_ROWS + HEAD_W, :] = vT[hh * HEAD_W:(hh + 1) * HEAD_W, :].astype(BF16)
        vT_ref[hh * V_ROWS + HEAD_W:(hh + 1) * V_ROWS, :] = ones


def _fft1_kernel(x_ref, f1_ref, twr_ref, twi_ref, tre_ref, tim_ref, *, d):
    n1 = x_ref.shape[0]
    t = _dot(f1_ref[...], x_ref[...])
    for q in range(x_ref.shape[1] // d):
        rep = d // GROUP_DIM
        wr = jnp.concatenate([twr_ref[:, q * GROUP_DIM:(q + 1) * GROUP_DIM]] * rep, axis=1)
        wi = jnp.concatenate([twi_ref[:, q * GROUP_DIM:(q + 1) * GROUP_DIM]] * rep, axis=1)
        a = t[:n1, q * d:(q + 1) * d]
        b = t[n1:, q * d:(q + 1) * d]
        tre_ref[:, q * d:(q + 1) * d] = (a * wr - b * wi).astype(BF16)
        tim_ref[:, q * d:(q + 1) * d] = (a * wi + b * wr).astype(BF16)


def _fft2_kernel(tre_ref, tim_ref, f2_ref, cs_ref, wfo_ref, y_ref, *, n2):
    d = tre_ref.shape[1]
    grp = tre_ref.shape[0] // n2
    f2 = f2_ref[...]
    zs = []
    for i in range(grp):
        rhs = jnp.concatenate([tre_ref[i * n2:(i + 1) * n2, :], tim_ref[i * n2:(i + 1) * n2, :]], axis=0)
        zs.append(_dot(f2, rhs))
    re = jnp.concatenate([z[:n2] for z in zs], axis=0).astype(BF16)
    im = jnp.concatenate([z[n2:] for z in zs], axis=0).astype(BF16)
    cs = cs_ref[...]
    ys = []
    for g in range(d // GROUP_DIM):
        lo, hi = g * GROUP_DIM, (g + 1) * GROUP_DIM
        ys.append(_dot(jnp.concatenate([re[:, lo:hi], im[:, lo:hi]], axis=1), cs))
    y = jnp.concatenate(ys, axis=1).astype(BF16)
    out = _dot(y, wfo_ref[...])
    for i in range(grp):
        y_ref[:, i * d:(i + 1) * d] = out[i * n2:(i + 1) * n2, :]


def _attn_kernel(cvec_ref, qT_ref, k_ref, vT_ref, augk_ref, augq_ref, lq1_ref, lk1_ref, lq2_ref,
                 lk2_ref, gs_ref, o_ref, w_ref, s_ref, m_ref, *, tk, nq, n_tiles):
    step = pl.program_id(0)
    tq = qT_ref.shape[1]
    s_len = k_ref.shape[0]
    nk = s_len // tk

    def chunk_max(s):
        return jnp.max(s.reshape(tk // 8, 8, tq), axis=0)

    def pass1_steps(slot, g):
        c_h = cvec_ref[(g // nq) % N_HEADS]
        j0 = (g % nq) * tq
        cd = j0 // tk
        qT = qT_ref[...]
        zq = jnp.zeros((HEAD_DIM, tq), BF16)
        qa = jnp.concatenate([qT[:HEAD_DIM], zq], axis=0)
        qb = jnp.concatenate([zq, qT[HEAD_DIM:]], axis=0)
        augq = augq_ref[0]
        naugq = -augq
        w_ref[0] = jnp.concatenate([qa, augq], axis=0)
        w_ref[1] = jnp.concatenate([qa, naugq], axis=0)
        w_ref[2] = jnp.concatenate([qb, augq], axis=0)
        w_ref[3] = jnp.concatenate([qb, naugq], axis=0)
        augk = augk_ref[0]
        m1 = jnp.full((8, tq), -1e30, F32)
        m2 = m1
        for ci in range(nk - 1):
            c = ci + (ci >= cd).astype(jnp.int32)
            after = (c > cd).astype(jnp.int32)
            k0 = pl.multiple_of(c * tk, tk)
            lhs = jnp.concatenate([k_ref[pl.ds(k0, tk), :], augk], axis=1)
            s1 = _dot(lhs, w_ref[after])
            s2 = _dot(lhs, w_ref[2 + after])
            s_ref[slot, 0, pl.ds(k0, tk), :] = s1
            s_ref[slot, 1, pl.ds(k0, tk), :] = s2
            off = c_h * ((k0 - j0) * (1 - 2 * after)).astype(F32)
            m1 = jnp.maximum(m1, chunk_max(s1) + off)
            m2 = jnp.maximum(m2, chunk_max(s2) + off)
            yield
        kd0 = pl.multiple_of(cd * tk, tk)
        kc = k_ref[pl.ds(kd0, tk), :]
        dpos = ((kd0 - j0) + lax.broadcasted_iota(jnp.int32, (tk, 1), 0)
                - lax.broadcasted_iota(jnp.int32, (1, tq), 1)).astype(F32)
        bias = -c_h * jnp.abs(dpos)
        s1 = _dot(kc, qa) + bias
        s2 = _dot(kc, qb) + bias
        s_ref[slot, 0, pl.ds(kd0, tk), :] = s1
        s_ref[slot, 1, pl.ds(kd0, tk), :] = s2
        m1 = jnp.max(jnp.maximum(m1, chunk_max(s1)), axis=0, keepdims=True)
        m2 = jnp.max(jnp.maximum(m2, chunk_max(s2)), axis=0, keepdims=True)
        m_ref[slot, 0] = jnp.broadcast_to(m1, (8, tq))
        m_ref[slot, 1] = jnp.broadcast_to(m2, (8, tq))
        yield

    def pass2_steps(slot, g):
        c_h = cvec_ref[(g // nq) % N_HEADS]
        j0 = (g % nq) * tq
        cd = j0 // tk
        m1 = m_ref[slot, 0][0:1]
        m2 = m_ref[slot, 1][0:1]
        acc1 = jnp.zeros((V_ROWS, tq), F32)
        acc2 = acc1
        pending = None
        for c in range(nk):
            side = jnp.where(c < cd, 1, jnp.where(c > cd, -1, 0))
            off = c_h * (side * (c * tk - j0)).astype(F32)
            p1 = jnp.exp2(s_ref[slot, 0, c * tk:(c + 1) * tk, :] - (m1 - off)).astype(BF16)
            p2 = jnp.exp2(s_ref[slot, 1, c * tk:(c + 1) * tk, :] - (m2 - off)).astype(BF16)
            if pending is not None:
                vc = vT_ref[:, (c - 1) * tk:c * tk]
                acc1 = acc1 + _dot(vc, pending[0])
                acc2 = acc2 + _dot(vc, pending[1])
            pending = (p1, p2)
            if c < nk - 1:
                yield
        vc = vT_ref[:, (nk - 1) * tk:nk * tk]
        acc1 = acc1 + _dot(vc, pending[0])
        acc2 = acc2 + _dot(vc, pending[1])
        lam = (jnp.exp(jnp.sum(lq1_ref[...] * lk1_ref[...], axis=1, keepdims=True))
               - jnp.exp(jnp.sum(lq2_ref[...] * lk2_ref[...], axis=1, keepdims=True))
               + LAMBDA_INIT)
        o1 = acc1[:HEAD_W] / acc1[HEAD_W:HEAD_W + 1]
        o2 = acc2[:HEAD_W] / acc2[HEAD_W:HEAD_W + 1]
        o = o1 - lam * o2
        on = o * lax.rsqrt(jnp.mean(o * o, axis=0, keepdims=True) + RMS_EPS)
        o_ref[...] = (on.T * (gs_ref[...] * (1.0 - LAMBDA_INIT))).astype(BF16)
        yield

    slot_new = step % 2
    slot_old = 1 - slot_new

    @pl.when(step == 0)
    def _():
        for _ in pass1_steps(slot_new, step):
            pass

    @pl.when(jnp.logical_and(step > 0, step < n_tiles))
    def _():
        for _ in zip(pass1_steps(slot_new, step), pass2_steps(slot_old, step - 1)):
            pass

    @pl.when(step == n_tiles)
    def _():
        for _ in pass2_steps(slot_old, step - 1):
            pass


def _alibi_tables(tk, tq):
    c = (2.0 ** (-8.0 * np.arange(1, N_HEADS + 1) / N_HEADS) * math.log2(math.e)).astype(np.float32)

    def split3(x):
        x1 = x.astype(BF16)
        r = x - x1.astype(np.float32)
        x2 = r.astype(BF16)
        x3 = (r - x2.astype(np.float32)).astype(BF16)
        return x1, x2, x3

    augk = np.zeros((N_HEADS, tk, HEAD_W), BF16)
    augq = np.zeros((N_HEADS, HEAD_W, tq), BF16)
    u = c[:, None] * np.arange(tk, dtype=np.float32)[None, :]
    v = c[:, None] * np.arange(tq, dtype=np.float32)[None, :]
    for n, (un, vn) in enumerate(zip(split3(u), split3(v))):
        augk[:, :, n] = un
        augk[:, :, 3 + n] = 1.0
        augq[:, n, :] = 1.0
        augq[:, 3 + n, :] = -vn
    return c, augk, augq


def _tail_kernel(x_ref, yf_ref, ya_ref, p_ref, gmix_ref, wbg_ref, bbg_ref, wao_ref, wo_ref,
                 gffn_ref, wg_ref, wu_ref, wd_ref, gple_ref, wpg_ref, wpp_ref, gfin_ref, out_ref):
    d = x_ref.shape[1]
    x = x_ref[...]
    h = _rms(x, gmix_ref[...]).astype(BF16)
    gates = jax.nn.sigmoid(_dot(h, wbg_ref[...]) + bbg_ref[...])
    ya = _dot(ya_ref[...], wao_ref[...])
    merged = gates[:, :d] * yf_ref[...] + gates[:, d:] * ya
    x = x + _dot(merged.astype(BF16), wo_ref[...])

    h2 = _rms(x, gffn_ref[...]).astype(BF16)
    gg = _dot(h2, wg_ref[...])
    uu = _dot(h2, wu_ref[...])
    act = (gg * jax.nn.sigmoid(gg)) * uu
    x = x + _dot(act.astype(BF16), wd_ref[...])

    h3 = _rms(x, gple_ref[...]).astype(BF16)
    pg = jax.nn.sigmoid(_dot(h3, wpg_ref[...]))
    x = x + pg * _dot(p_ref[...].astype(BF16), wpp_ref[...])
    out_ref[...] = _rms(x, gfin_ref[...])


def _const_spec(shape):
    return pl.BlockSpec(shape, lambda *_: (0,) * len(shape), pipeline_mode=pl.Buffered(1))


def _dft_tables(s_len):
    n1 = math.isqrt(s_len)
    n2 = s_len // n1
    assert n1 * n2 == s_len and n1 % 16 == 0 and n2 % 16 == 0

    def cos_sin(m, size):
        ang = 2.0 * np.pi * (m % size) / size
        return np.cos(ang), np.sin(ang)

    c1, s1 = cos_sin(np.outer(np.arange(n1), np.arange(n1)), n1)
    f1 = (np.concatenate([c1, -s1], axis=0) * n1 ** -0.5).astype(np.float32)
    ctw, stw = cos_sin(np.outer(np.arange(n1), np.arange(n2)), s_len)
    twr = np.repeat(ctw, GROUP_DIM, axis=1).astype(np.float32)
    twi = np.repeat(-stw, GROUP_DIM, axis=1).astype(np.float32)
    c2, s2 = cos_sin(np.outer(np.arange(n2), np.arange(n2)), n2)
    f2 = (np.block([[c2, s2], [-s2, c2]]) * n2 ** -0.5).astype(np.float32)
    cc, sc = cos_sin(np.outer(np.arange(GROUP_DIM), np.arange(GROUP_DIM)), GROUP_DIM)
    cs = (np.concatenate([cc, sc], axis=0) * GROUP_DIM ** -0.5).astype(np.float32)
    return n1, n2, f1, twr, twi, f2, cs


def kernel(x, p, g_mix, w_in, lambda_q1, lambda_k1, lambda_q2, lambda_k2, g_subln,
           w_fourier_out, w_attn_out, w_branch_gate, b_branch_gate, w_o, g_ffn, w_ffn_gate,
           w_ffn_up, w_ffn_down, g_ple, w_ple_gate, w_ple_proj, g_final):
    bsz, s_len, d = x.shape
    t_tok = bsz * s_len
    d_ff = w_ffn_gate.shape[-1]
    ple = p.shape[-1]
    assert w_in.shape[0] == 1 and d == N_HEADS * HEAD_W
    x2 = x.reshape(t_tok, d)
    p2 = p[0].reshape(t_tok, ple)
    n1, n2, f1, twr, twi, f2, cs = _dft_tables(s_len)
    cparams = functools.partial(pltpu.CompilerParams, vmem_limit_bytes=VMEM_LIMIT)

    tm = min(512, s_len)
    row = lambda i: (i, 0)
    col = lambda i: (0, i)
    u_f, qT, k_arr, vT = pl.pallas_call(
        _in_proj_kernel,
        grid=(t_tok // tm,),
        in_specs=[pl.BlockSpec((tm, d), row), _const_spec((1, d)), _const_spec((d, 4 * d))],
        out_specs=[pl.BlockSpec((tm, d), row), pl.BlockSpec((d, tm), col),
                   pl.BlockSpec((tm, d), row), pl.BlockSpec((N_HEADS * V_ROWS, tm), col)],
        out_shape=[jax.ShapeDtypeStruct((t_tok, d), BF16), jax.ShapeDtypeStruct((d, t_tok), BF16),
                   jax.ShapeDtypeStruct((t_tok, d), BF16),
                   jax.ShapeDtypeStruct((N_HEADS * V_ROWS, t_tok), BF16)],
        compiler_params=cparams(dimension_semantics=("arbitrary",)),
        name="in_proj",
    )(x2, g_mix[0][None, :], w_in[0].astype(BF16))

    qn = min(FFT1_COLS, n2)
    blk = pl.BlockSpec((n1, qn * d), lambda b, j: (b, j))
    t_re, t_im = pl.pallas_call(
        functools.partial(_fft1_kernel, d=d),
        grid=(bsz, n2 // qn),
        in_specs=[blk, _const_spec((2 * n1, n1)),
                  pl.BlockSpec((n1, qn * GROUP_DIM), lambda b, j: (0, j)),
                  pl.BlockSpec((n1, qn * GROUP_DIM), lambda b, j: (0, j))],
        out_specs=[blk, blk],
        out_shape=[jax.ShapeDtypeStruct((bsz * n1, n2 * d), BF16)] * 2,
        compiler_params=cparams(dimension_semantics=("arbitrary", "arbitrary")),
        name="fft_stage1",
    )(u_f.reshape(bsz * n1, n2 * d), jnp.asarray(f1).astype(BF16), jnp.asarray(twr), jnp.asarray(twi))

    grp = min(FFT2_GROUP, n1)
    tblk = pl.BlockSpec((grp * n2, d), row)
    y_f = pl.pallas_call(
        functools.partial(_fft2_kernel, n2=n2),
        grid=(bsz * n1 // grp,),
        in_specs=[tblk, tblk, _const_spec((2 * n2, 2 * n2)),
                  _const_spec((2 * GROUP_DIM, GROUP_DIM)), _const_spec((d, d))],
        out_specs=pl.BlockSpec((n2, grp * d), lambda s: (s // (n1 // grp), s % (n1 // grp))),
        out_shape=jax.ShapeDtypeStruct((bsz * n2, n1 * d), F32),
        compiler_params=cparams(dimension_semantics=("arbitrary",)),
        name="fft_stage2",
    )(t_re.reshape(t_tok, d), t_im.reshape(t_tok, d), jnp.asarray(f2).astype(BF16),
      jnp.asarray(cs).astype(BF16),
      w_fourier_out[0].astype(BF16))
    y_f = y_f.reshape(t_tok, d)

    tq = min(256, s_len)
    tk = min(512, s_len)
    nq = s_len // tq
    cvec, augk, augq = _alibi_tables(tk, tq)
    n_tiles = bsz * N_HEADS * nq

    def tile(g):
        return g // (N_HEADS * nq), (g // nq) % N_HEADS, g % nq

    cur = lambda g: tile(jnp.minimum(g, n_tiles - 1))
    prev = lambda g: tile(jnp.maximum(g - 1, 0))
    lam_spec = pl.BlockSpec((1, HEAD_DIM), lambda g: (0, 0))
    y_a = pl.pallas_call(
        functools.partial(_attn_kernel, tk=tk, nq=nq, n_tiles=n_tiles),
        grid=(n_tiles + 1,),
        in_specs=[pl.BlockSpec(memory_space=pltpu.SMEM),
                  pl.BlockSpec((HEAD_W, tq), lambda g: (cur(g)[1], cur(g)[0] * nq + cur(g)[2])),
                  pl.BlockSpec((s_len, HEAD_W), lambda g: (cur(g)[0], cur(g)[1])),
                  pl.BlockSpec((V_ROWS, s_len), lambda g: (prev(g)[1], prev(g)[0])),
                  pl.BlockSpec((1, tk, HEAD_W), lambda g: (cur(g)[1], 0, 0)),
                  pl.BlockSpec((1, HEAD_W, tq), lambda g: (cur(g)[1], 0, 0)),
                  lam_spec, lam_spec, lam_spec, lam_spec,
                  pl.BlockSpec((1, HEAD_W), lambda g: (0, 0))],
        out_specs=pl.BlockSpec((tq, HEAD_W), lambda g: (prev(g)[0] * nq + prev(g)[2], prev(g)[1])),
        out_shape=jax.ShapeDtypeStruct((t_tok, d), BF16),
        scratch_shapes=[pltpu.VMEM((4, 2 * HEAD_W, tq), BF16),
                        pltpu.VMEM((2, 2, s_len, tq), F32), pltpu.VMEM((2, 2, 8, tq), F32)],
        compiler_params=cparams(dimension_semantics=("arbitrary",)),
        name="diff_attn",
    )(jnp.asarray(cvec), qT, k_arr, vT, jnp.asarray(augk), jnp.asarray(augq),
      lambda_q1, lambda_k1, lambda_q2, lambda_k2, g_subln)

    tt = min(256, s_len)
    vec = lambda a: a.reshape(1, -1)
    out = pl.pallas_call(
        _tail_kernel,
        grid=(t_tok // tt,),
        in_specs=[pl.BlockSpec((tt, d), row), pl.BlockSpec((tt, d), row),
                  pl.BlockSpec((tt, d), row), pl.BlockSpec((tt, ple), row),
                  _const_spec((1, d)), _const_spec((d, 2 * d)), _const_spec((1, 2 * d)),
                  _const_spec((d, d)), _const_spec((d, d)),
                  _const_spec((1, d)), _const_spec((d, d_ff)), _const_spec((d, d_ff)),
                  _const_spec((d_ff, d)),
                  _const_spec((1, d)), _const_spec((d, d)), _const_spec((ple, d)),
                  _const_spec((1, d))],
        out_specs=pl.BlockSpec((tt, d), row),
        out_shape=jax.ShapeDtypeStruct((t_tok, d), F32),
        compiler_params=cparams(dimension_semantics=("arbitrary",)),
        name="tail",
    )(x2, y_f, y_a, p2, vec(g_mix[0]), w_branch_gate[0].astype(BF16), vec(b_branch_gate[0]),
      w_attn_out[0].astype(BF16), w_o[0].astype(BF16), vec(g_ffn[0]),
      w_ffn_gate[0].astype(BF16), w_ffn_up[0].astype(BF16), w_ffn_down[0].astype(BF16),
      vec(g_ple[0]), w_ple_gate[0].astype(BF16), w_ple_proj[0].astype(BF16), vec(g_final))
    return out.reshape(bsz, s_len, d)
```

```python
import functools
import math
from typing import Any, NamedTuple

import jax
import jax.numpy as jnp
import numpy as np
from jax import lax
from jax.experimental import pallas as pl
from jax.experimental.pallas import tpu as pltpu

F32 = jnp.float32
BF16 = jnp.bfloat16

N_HEADS = 8
HEAD_DIM = 64
HEAD_W = 2 * HEAD_DIM
GROUP_DIM = 128
RMS_EPS = 1e-6
LAMBDA_INIT = 0.8 - 0.6 * math.exp(-0.3 * 0)
V_ROWS = HEAD_W + 16
VMEM_LIMIT = 56 * 1024 * 1024
FFT1_COLS = 8
FFT2_GROUP = 8
PV_LAG = 1
PASS2_LEAD = 1
OWN_CHUNK_AT = 3
Q_SCALE = HEAD_DIM ** -0.5 * math.log2(math.e)


def _rms(x, g):
    return x * lax.rsqrt(jnp.mean(x * x, axis=-1, keepdims=True) + RMS_EPS) * g


def _dot(a, b):
    return jnp.dot(a, b, preferred_element_type=F32)


def _in_proj_kernel(x_ref, g_ref, w_ref, uf_ref, qT_ref, k_ref, vT_ref):
    d = x_ref.shape[1]
    tm = x_ref.shape[0]
    h = _rms(x_ref[...], g_ref[...]).astype(BF16)
    uf_ref[...] = _dot(h, w_ref[:, 0:d]).astype(BF16)
    q = _dot(h, w_ref[:, d:2 * d]) * Q_SCALE
    qT_ref[...] = q.T.astype(BF16)
    k_ref[...] = _dot(h, w_ref[:, 2 * d:3 * d]).astype(BF16)
    vT = _dot(h, w_ref[:, 3 * d:4 * d]).T
    ones = jnp.ones((V_ROWS - HEAD_W, tm), BF16)
    for hh in range(N_HEADS):
        vT_ref[hh * V_ROWS:hh * V_ROWS + HEAD_W, :] = vT[hh * HEAD_W:(hh + 1) * HEAD_W, :].astype(BF16)
        vT_ref[hh * V_ROWS + HEAD_W:(hh + 1) * V_ROWS, :] = ones


def _fft1_kernel(x_ref, f1_ref, twr_ref, twi_ref, tre_ref, tim_ref, *, d):
    n1 = x_ref.shape[0]
    t = _dot(f1_ref[...], x_ref[...])
    for q in range(x_ref.shape[1] // d):
        rep = d // GROUP_DIM
        wr = jnp.concatenate([twr_ref[:, q * GROUP_DIM:(q + 1) * GROUP_DIM]] * rep, axis=1)
        wi = jnp.concatenate([twi_ref[:, q * GROUP_DIM:(q + 1) * GROUP_DIM]] * rep, axis=1)
        a = t[:n1, q * d:(q + 1) * d]
        b = t[n1:, q * d:(q + 1) * d]
        tre_ref[:, q * d:(q + 1) * d] = (a * wr - b * wi).astype(BF16)
        tim_ref[:, q * d:(q + 1) * d] = (a * wi + b * wr).astype(BF16)


def _fft2_kernel(tre_ref, tim_ref, f2_ref, cs_ref, wfo_ref, y_ref, *, n2):
    d = tre_ref.shape[1]
    grp = tre_ref.shape[0] // n2
    f2 = f2_ref[...]
    zs = []
    for i in range(grp):
        rhs = jnp.concatenate([tre_ref[i * n2:(i + 1) * n2, :], tim_ref[i * n2:(i + 1) * n2, :]], axis=0)
        zs.append(_dot(f2, rhs))
    re = jnp.concatenate([z[:n2] for z in zs], axis=0).astype(BF16)
    im = jnp.concatenate([z[n2:] for z in zs], axis=0).astype(BF16)
    cs = cs_ref[...]
    ys = []
    for g in range(d // GROUP_DIM):
        lo, hi = g * GROUP_DIM, (g + 1) * GROUP_DIM
        ys.append(_dot(jnp.concatenate([re[:, lo:hi], im[:, lo:hi]], axis=1), cs))
    y = jnp.concatenate(ys, axis=1).astype(BF16)
    out = _dot(y, wfo_ref[...])
    for i in range(grp):
        y_ref[:, i * d:(i + 1) * d] = out[i * n2:(i + 1) * n2, :]


class _Slot(NamedTuple):
    s: Any
    m: Any


def _attn_kernel(cvec_ref, qT_ref, k_ref, vT_ref, augk_ref, augq_ref, bias_ref, lq1_ref, lk1_ref, lq2_ref,
                 lk2_ref, gs_ref, o_ref, sa_ref, sb_ref, ma_ref, mb_ref, *, tk, nq, n_tiles):
    step = pl.program_id(0)
    tq = qT_ref.shape[1]
    s_len = k_ref.shape[0]
    nk = s_len // tk

    def chunk_max(s):
        return jnp.max(s.reshape(tk // 8, 8, tq), axis=0)

    def pass1_steps(slot, g):
        c_h = cvec_ref[(g // nq) % N_HEADS]
        j0 = (g % nq) * tq
        cd = j0 // tk
        qT = qT_ref[...]
        zq = jnp.zeros((HEAD_DIM, tq), BF16)
        qa = jnp.concatenate([qT[:HEAD_DIM], zq], axis=0)
        qb = jnp.concatenate([zq, qT[HEAD_DIM:]], axis=0)
        augq = augq_ref[0]
        w1 = jnp.concatenate([qa, augq], axis=0)
        w2 = jnp.concatenate([qb, augq], axis=0)
        m1 = jnp.full((8, tq), -1e30, F32)
        m2 = m1
        others = iter(range(nk - 1))
        for n in range(nk):
            if n == min(OWN_CHUNK_AT, nk - 1):
                k0 = pl.multiple_of(cd * tk, tk)
                bias = bias_ref[0, 0]
                kc = k_ref[pl.ds(k0, tk), :]
                s1 = _dot(kc, qa) + bias
                s2 = _dot(kc, qb) + bias
                off = 0.0
            else:
                ci = next(others)
                c = ci + (ci >= cd).astype(jnp.int32)
                after = (c > cd).astype(jnp.int32)
                k0 = pl.multiple_of(c * tk, tk)
                lhs = jnp.concatenate([k_ref[pl.ds(k0, tk), :], augk_ref[after]], axis=1)
                s1 = _dot(lhs, w1)
                s2 = _dot(lhs, w2)
                off = c_h * ((k0 - j0) * (1 - 2 * after)).astype(F32)
            slot.s[0, pl.ds(k0, tk), :] = s1
            slot.s[1, pl.ds(k0, tk), :] = s2
            m1 = jnp.maximum(m1, chunk_max(s1) + off)
            m2 = jnp.maximum(m2, chunk_max(s2) + off)
            if n == nk - 1:
                slot.m[0] = jnp.broadcast_to(jnp.max(m1, axis=0, keepdims=True), (8, tq))
                slot.m[1] = jnp.broadcast_to(jnp.max(m2, axis=0, keepdims=True), (8, tq))
            yield

    def pass2_steps(slot, g):
        c_h = cvec_ref[(g // nq) % N_HEADS]
        j0 = (g % nq) * tq
        cd = j0 // tk
        m1 = slot.m[0][0:1]
        m2 = slot.m[1][0:1]
        acc1 = jnp.zeros((V_ROWS, tq), F32)
        acc2 = acc1
        pending = []

        def value_matmul(acc1, acc2):
            cp, p1, p2 = pending.pop(0)
            vc = vT_ref[:, cp * tk:(cp + 1) * tk]
            return acc1 + _dot(vc, p1), acc2 + _dot(vc, p2)

        for c in range(nk):
            side = jnp.where(c < cd, 1, jnp.where(c > cd, -1, 0))
            off = c_h * (side * (c * tk - j0)).astype(F32)
            p1 = jnp.exp2(slot.s[0, c * tk:(c + 1) * tk, :] - (m1 - off)).astype(BF16)
            p2 = jnp.exp2(slot.s[1, c * tk:(c + 1) * tk, :] - (m2 - off)).astype(BF16)
            pending.append((c, p1, p2))
            if len(pending) > PV_LAG:
                acc1, acc2 = value_matmul(acc1, acc2)
            if c < nk - 1:
                yield
        while pending:
            acc1, acc2 = value_matmul(acc1, acc2)
        lam = (jnp.exp(jnp.sum(lq1_ref[...] * lk1_ref[...], axis=1, keepdims=True))
               - jnp.exp(jnp.sum(lq2_ref[...] * lk2_ref[...], axis=1, keepdims=True))
               + LAMBDA_INIT)
        o1 = acc1[:HEAD_W] / acc1[HEAD_W:HEAD_W + 1]
        o2 = acc2[:HEAD_W] / acc2[HEAD_W:HEAD_W + 1]
        o = o1 - lam * o2
        on = o * lax.rsqrt(jnp.mean(o * o, axis=0, keepdims=True) + RMS_EPS)
        o_ref[...] = (on.T * (gs_ref[...] * (1.0 - LAMBDA_INIT))).astype(BF16)
        yield

    slots = (_Slot(sa_ref, ma_ref), _Slot(sb_ref, mb_ref))
    inner = jnp.logical_and(step > 0, step < n_tiles)

    @pl.when(step == 0)
    def _():
        for _ in pass1_steps(slots[0], step):
            pass

    for parity in range(2):
        @pl.when(jnp.logical_and(inner, step % 2 == parity))
        def _():
            older = pass2_steps(slots[1 - parity], step - 1)
            newer = pass1_steps(slots[parity], step)
            for _ in range(PASS2_LEAD):
                next(older)
            for _ in range(nk - PASS2_LEAD):
                next(older)
                next(newer)
            for _ in range(PASS2_LEAD):
                next(newer)

    @pl.when(step == n_tiles)
    def _():
        for _ in pass2_steps(slots[(n_tiles - 1) % 2], step - 1):
            pass


def _alibi_tables(tk, tq):
    c = (2.0 ** (-8.0 * np.arange(1, N_HEADS + 1) / N_HEADS) * math.log2(math.e)).astype(np.float32)

    def split3(x):
        x1 = x.astype(BF16)
        r = x - x1.astype(np.float32)
        x2 = r.astype(BF16)
        x3 = (r - x2.astype(np.float32)).astype(BF16)
        return x1, x2, x3

    augk = np.zeros((N_HEADS, 2, tk, HEAD_W), BF16)
    augq = np.zeros((N_HEADS, HEAD_W, tq), BF16)
    u = c[:, None] * np.arange(tk, dtype=np.float32)[None, :]
    v = c[:, None] * np.arange(tq, dtype=np.float32)[None, :]
    for n, (un, vn) in enumerate(zip(split3(u), split3(v))):
        augk[:, 0, :, n] = un
        augk[:, 0, :, 3 + n] = 1.0
        augq[:, n, :] = 1.0
        augq[:, 3 + n, :] = -vn
    augk[:, 1] = -augk[:, 0]
    pos = np.arange(tk // tq, dtype=np.float32)[:, None, None] * tq
    dist = np.abs(np.arange(tk, dtype=np.float32)[None, :, None]
                  - np.arange(tq, dtype=np.float32)[None, None, :] - pos)
    bias = (-c[:, None, None, None] * dist[None]).astype(np.float32)
    return c, augk.reshape(2 * N_HEADS, tk, HEAD_W), augq, bias


def _tail_kernel(x_ref, yf_ref, ya_ref, p_ref, gmix_ref, wbg_ref, bbg_ref, wao_ref, wo_ref,
                 gffn_ref, wg_ref, wu_ref, wd_ref, gple_ref, wpg_ref, wpp_ref, gfin_ref, out_ref):
    d = x_ref.shape[1]
    x = x_ref[...]
    h = _rms(x, gmix_ref[...]).astype(BF16)
    gates = jax.nn.sigmoid(_dot(h, wbg_ref[...]) + bbg_ref[...])
    ya = _dot(ya_ref[...], wao_ref[...])
    merged = gates[:, :d] * yf_ref[...] + gates[:, d:] * ya
    x = x + _dot(merged.astype(BF16), wo_ref[...])

    h2 = _rms(x, gffn_ref[...]).astype(BF16)
    gg = _dot(h2, wg_ref[...])
    uu = _dot(h2, wu_ref[...])
    act = (gg * jax.nn.sigmoid(gg)) * uu
    x = x + _dot(act.astype(BF16), wd_ref[...])

    h3 = _rms(x, gple_ref[...]).astype(BF16)
    pg = jax.nn.sigmoid(_dot(h3, wpg_ref[...]))
    x = x + pg * _dot(p_ref[...].astype(BF16), wpp_ref[...])
    out_ref[...] = _rms(x, gfin_ref[...])


def _const_spec(shape):
    return pl.BlockSpec(shape, lambda *_: (0,) * len(shape), pipeline_mode=pl.Buffered(1))


def _dft_tables(s_len):
    n1 = math.isqrt(s_len)
    n2 = s_len // n1
    assert n1 * n2 == s_len and n1 % 16 == 0 and n2 % 16 == 0

    def cos_sin(m, size):
        ang = 2.0 * np.pi * (m % size) / size
        return np.cos(ang), np.sin(ang)

    c1, s1 = cos_sin(np.outer(np.arange(n1), np.arange(n1)), n1)
    f1 = (np.concatenate([c1, -s1], axis=0) * n1 ** -0.5).astype(np.float32)
    ctw, stw = cos_sin(np.outer(np.arange(n1), np.arange(n2)), s_len)
    twr = np.repeat(ctw, GROUP_DIM, axis=1).astype(np.float32)
    twi = np.repeat(-stw, GROUP_DIM, axis=1).astype(np.float32)
    c2, s2 = cos_sin(np.outer(np.arange(n2), np.arange(n2)), n2)
    f2 = (np.block([[c2, s2], [-s2, c2]]) * n2 ** -0.5).astype(np.float32)
    cc, sc = cos_sin(np.outer(np.arange(GROUP_DIM), np.arange(GROUP_DIM)), GROUP_DIM)
    cs = (np.concatenate([cc, sc], axis=0) * GROUP_DIM ** -0.5).astype(np.float32)
    return n1, n2, f1, twr, twi, f2, cs


def kernel(x, p, g_mix, w_in, lambda_q1, lambda_k1, lambda_q2, lambda_k2, g_subln,
           w_fourier_out, w_attn_out, w_branch_gate, b_branch_gate, w_o, g_ffn, w_ffn_gate,
           w_ffn_up, w_ffn_down, g_ple, w_ple_gate, w_ple_proj, g_final):
    bsz, s_len, d = x.shape
    t_tok = bsz * s_len
    d_ff = w_ffn_gate.shape[-1]
    ple = p.shape[-1]
    assert w_in.shape[0] == 1 and d == N_HEADS * HEAD_W
    x2 = x.reshape(t_tok, d)
    p2 = p[0].reshape(t_tok, ple)
    n1, n2, f1, twr, twi, f2, cs = _dft_tables(s_len)
    cparams = functools.partial(pltpu.CompilerParams, vmem_limit_bytes=VMEM_LIMIT)

    tm = min(512, s_len)
    row = lambda i: (i, 0)
    col = lambda i: (0, i)
    u_f, qT, k_arr, vT = pl.pallas_call(
        _in_proj_kernel,
        grid=(t_tok // tm,),
        in_specs=[pl.BlockSpec((tm, d), row), _const_spec((1, d)), _const_spec((d, 4 * d))],
        out_specs=[pl.BlockSpec((tm, d), row), pl.BlockSpec((d, tm), col),
                   pl.BlockSpec((tm, d), row), pl.BlockSpec((N_HEADS * V_ROWS, tm), col)],
        out_shape=[jax.ShapeDtypeStruct((t_tok, d), BF16), jax.ShapeDtypeStruct((d, t_tok), BF16),
                   jax.ShapeDtypeStruct((t_tok, d), BF16),
                   jax.ShapeDtypeStruct((N_HEADS * V_ROWS, t_tok), BF16)],
        compiler_params=cparams(dimension_semantics=("arbitrary",)),
        name="in_proj",
    )(x2, g_mix[0][None, :], w_in[0].astype(BF16))

    qn = min(FFT1_COLS, n2)
    blk = pl.BlockSpec((n1, qn * d), lambda b, j: (b, j))
    t_re, t_im = pl.pallas_call(
        functools.partial(_fft1_kernel, d=d),
        grid=(bsz, n2 // qn),
        in_specs=[blk, _const_spec((2 * n1, n1)),
                  pl.BlockSpec((n1, qn * GROUP_DIM), lambda b, j: (0, j)),
                  pl.BlockSpec((n1, qn * GROUP_DIM), lambda b, j: (0, j))],
        out_specs=[blk, blk],
        out_shape=[jax.ShapeDtypeStruct((bsz * n1, n2 * d), BF16)] * 2,
        compiler_params=cparams(dimension_semantics=("arbitrary", "arbitrary")),
        name="fft_stage1",
    )(u_f.reshape(bsz * n1, n2 * d), jnp.asarray(f1).astype(BF16), jnp.asarray(twr), jnp.asarray(twi))

    grp = min(FFT2_GROUP, n1)
    tblk = pl.BlockSpec((grp * n2, d), row)
    y_f = pl.pallas_call(
        functools.partial(_fft2_kernel, n2=n2),
        grid=(bsz * n1 // grp,),
        in_specs=[tblk, tblk, _const_spec((2 * n2, 2 * n2)),
                  _const_spec((2 * GROUP_DIM, GROUP_DIM)), _const_spec((d, d))],
        out_specs=pl.BlockSpec((n2, grp * d), lambda s: (s // (n1 // grp), s % (n1 // grp))),
        out_shape=jax.ShapeDtypeStruct((bsz * n2, n1 * d), F32),
        compiler_params=cparams(dimension_semantics=("arbitrary",)),
        name="fft_stage2",
    )(t_re.reshape(t_tok, d), t_im.reshape(t_tok, d), jnp.asarray(f2).astype(BF16),
      jnp.asarray(cs).astype(BF16),
      w_fourier_out[0].astype(BF16))
    y_f = y_f.reshape(t_tok, d)

    tq = min(256, s_len)
    tk = min(512, s_len)
    nq = s_len // tq
    cvec, augk, augq, bias = _alibi_tables(tk, tq)
    n_tiles = bsz * N_HEADS * nq

    def tile(g):
        return g // (N_HEADS * nq), (g // nq) % N_HEADS, g % nq

    cur = lambda g: tile(jnp.minimum(g, n_tiles - 1))
    prev = lambda g: tile(jnp.maximum(g - 1, 0))
    lam_spec = pl.BlockSpec((1, HEAD_DIM), lambda g: (0, 0))
    y_a = pl.pallas_call(
        functools.partial(_attn_kernel, tk=tk, nq=nq, n_tiles=n_tiles),
        grid=(n_tiles + 1,),
        in_specs=[pl.BlockSpec(memory_space=pltpu.SMEM),
                  pl.BlockSpec((HEAD_W, tq), lambda g: (cur(g)[1], cur(g)[0] * nq + cur(g)[2])),
                  pl.BlockSpec((s_len, HEAD_W), lambda g: (cur(g)[0], cur(g)[1])),
                  pl.BlockSpec((V_ROWS, s_len), lambda g: (prev(g)[1], prev(g)[0])),
                  pl.BlockSpec((2, tk, HEAD_W), lambda g: (cur(g)[1], 0, 0)),
                  pl.BlockSpec((1, HEAD_W, tq), lambda g: (cur(g)[1], 0, 0)),
                  pl.BlockSpec((1, 1, tk, tq), lambda g: (cur(g)[1], cur(g)[2] % (tk // tq), 0, 0)),
                  lam_spec, lam_spec, lam_spec, lam_spec,
                  pl.BlockSpec((1, HEAD_W), lambda g: (0, 0))],
        out_specs=pl.BlockSpec((tq, HEAD_W), lambda g: (prev(g)[0] * nq + prev(g)[2], prev(g)[1])),
        out_shape=jax.ShapeDtypeStruct((t_tok, d), BF16),
        scratch_shapes=[pltpu.VMEM((2, s_len, tq), F32), pltpu.VMEM((2, s_len, tq), F32),
                        pltpu.VMEM((2, 8, tq), F32), pltpu.VMEM((2, 8, tq), F32)],
        compiler_params=cparams(dimension_semantics=("arbitrary",)),
        name="diff_attn",
    )(jnp.asarray(cvec), qT, k_arr, vT, jnp.asarray(augk), jnp.asarray(augq), jnp.asarray(bias),
      lambda_q1, lambda_k1, lambda_q2, lambda_k2, g_subln)

    tt = min(256, s_len)
    vec = lambda a: a.reshape(1, -1)
    out = pl.pallas_call(
        _tail_kernel,
        grid=(t_tok // tt,),
        in_specs=[pl.BlockSpec((tt, d), row), pl.BlockSpec((tt, d), row),
                  pl.BlockSpec((tt, d), row), pl.BlockSpec((tt, ple), row),
                  _const_spec((1, d)), _const_spec((d, 2 * d)), _const_spec((1, 2 * d)),
                  _const_spec((d, d)), _const_spec((d, d)),
                  _const_spec((1, d)), _const_spec((d, d_ff)), _const_spec((d, d_ff)),
                  _const_spec((d_ff, d)),
                  _const_spec((1, d)), _const_spec((d, d)), _const_spec((ple, d)),
                  _const_spec((1, d))],
        out_specs=pl.BlockSpec((tt, d), row),
        out_shape=jax.ShapeDtypeStruct((t_tok, d), F32),
        compiler_params=cparams(dimension_semantics=("arbitrary",)),
        name="tail",
    )(x2, y_f, y_a, p2, vec(g_mix[0]), w_branch_gate[0].astype(BF16), vec(b_branch_gate[0]),
      w_attn_out[0].astype(BF16), w_o[0].astype(BF16), vec(g_ffn[0]),
      w_ffn_gate[0].astype(BF16), w_ffn_up[0].astype(BF16), w_ffn_down[0].astype(BF16),
      vec(g_ple[0]), w_ple_gate[0].astype(BF16), w_ple_proj[0].astype(BF16), vec(g_final))
    return out.reshape(bsz, s_len, d)
```

```python
import functools
import math
from typing import Any, NamedTuple

import jax
import jax.numpy as jnp
import numpy as np
from jax import lax
from jax.experimental import pallas as pl
from jax.experimental.pallas import tpu as pltpu

F32 = jnp.float32
BF16 = jnp.bfloat16

N_HEADS = 8
HEAD_DIM = 64
HEAD_W = 2 * HEAD_DIM
GROUP_DIM = 128
RMS_EPS = 1e-6
LAMBDA_INIT = 0.8 - 0.6 * math.exp(-0.3 * 0)
V_ROWS = HEAD_W + 16
VMEM_LIMIT = 56 * 1024 * 1024
FFT1_ROWS = 16
FFT2_GROUP = 8
PV_LAG = 1
PASS2_LEAD = 1
OWN_CHUNK_AT = 3
Q_SCALE = HEAD_DIM ** -0.5 * math.log2(math.e)


def _rms(x, g):
    return x * lax.rsqrt(jnp.mean(x * x, axis=-1, keepdims=True) + RMS_EPS) * g


def _dot(a, b):
    return jnp.dot(a, b, preferred_element_type=F32)


def _divmod_pow2(x, n):
    assert n > 0 and n & (n - 1) == 0, n
    return x >> (n.bit_length() - 1), x & (n - 1)


def _in_proj_kernel(x_ref, g_ref, w_ref, uf_ref, qT_ref, k_ref, vT_ref):
    d = x_ref.shape[1]
    tm = x_ref.shape[0]
    h = _rms(x_ref[...], g_ref[...]).astype(BF16)
    uf_ref[...] = _dot(h, w_ref[:, 0:d])
    q = _dot(h, w_ref[:, d:2 * d]) * Q_SCALE
    qT_ref[...] = q.T.astype(BF16)
    k_ref[...] = _dot(h, w_ref[:, 2 * d:3 * d]).astype(BF16)
    vT = _dot(h, w_ref[:, 3 * d:4 * d]).T
    ones = jnp.ones((V_ROWS - HEAD_W, tm), BF16)
    for hh in range(N_HEADS):
        vT_ref[hh * V_ROWS:hh * V_ROWS + HEAD_W, :] = vT[hh * HEAD_W:(hh + 1) * HEAD_W, :].astype(BF16)
        vT_ref[hh * V_ROWS + HEAD_W:(hh + 1) * V_ROWS, :] = ones


def _fft1_kernel(x_ref, f1k_ref, twr_ref, twi_ref, tre_ref, tim_ref):
    n1, r, d = x_ref.shape
    x = x_ref[...].reshape(n1 * r, d).astype(BF16)
    t = _dot(f1k_ref[...], x)
    wr = twr_ref[0]
    wi = twi_ref[0]
    for cc in range(d // GROUP_DIM):
        lo, hi = cc * GROUP_DIM, (cc + 1) * GROUP_DIM
        a = t[:n1 * r, lo:hi]
        b = t[n1 * r:, lo:hi]
        tre_ref[:, :, lo:hi] = (a * wr - b * wi).astype(BF16).reshape(n1, r, GROUP_DIM)
        tim_ref[:, :, lo:hi] = (a * wi + b * wr).astype(BF16).reshape(n1, r, GROUP_DIM)


def _fft2_kernel(tre_ref, tim_ref, f2_ref, cs_ref, wfo_ref, y_ref, *, n2):
    d = tre_ref.shape[1]
    grp = tre_ref.shape[0] // n2
    f2 = f2_ref[...]
    zs = []
    for i in range(grp):
        rhs = jnp.concatenate([tre_ref[i * n2:(i + 1) * n2, :], tim_ref[i * n2:(i + 1) * n2, :]], axis=0)
        zs.append(_dot(f2, rhs))
    re = jnp.concatenate([z[:n2] for z in zs], axis=0).astype(BF16)
    im = jnp.concatenate([z[n2:] for z in zs], axis=0).astype(BF16)
    cs = cs_ref[...]
    ys = []
    for g in range(d // GROUP_DIM):
        lo, hi = g * GROUP_DIM, (g + 1) * GROUP_DIM
        ys.append(_dot(jnp.concatenate([re[:, lo:hi], im[:, lo:hi]], axis=1), cs))
    y = jnp.concatenate(ys, axis=1).astype(BF16)
    out = _dot(y, wfo_ref[...])
    for i in range(grp):
        y_ref[:, i * d:(i + 1) * d] = out[i * n2:(i + 1) * n2, :]


class _Slot(NamedTuple):
    s: Any
    m: Any


def _attn_kernel(cvec_ref, qT_ref, k_ref, vT_ref, augk_ref, augq_ref, bias_ref, lq1_ref, lk1_ref, lq2_ref,
                 lk2_ref, gs_ref, o_ref, sa_ref, sb_ref, ma_ref, mb_ref, *, tk, nq, n_tiles):
    step = pl.program_id(0)
    tq = qT_ref.shape[1]
    s_len = k_ref.shape[0]
    nk = s_len // tk

    def tile_scalars(g):
        gh, t = _divmod_pow2(g, nq)
        j0 = t * tq
        return cvec_ref[_divmod_pow2(gh, N_HEADS)[1]], j0, _divmod_pow2(j0, tk)[0]

    def chunk_max(s):
        return jnp.max(s.reshape(tk // 8, 8, tq), axis=0)

    def pass1_steps(slot, g):
        c_h, j0, cd = tile_scalars(g)
        qT = qT_ref[...]
        zq = jnp.zeros((HEAD_DIM, tq), BF16)
        qa = jnp.concatenate([qT[:HEAD_DIM], zq], axis=0)
        qb = jnp.concatenate([zq, qT[HEAD_DIM:]], axis=0)
        augq = augq_ref[0]
        w1 = jnp.concatenate([qa, augq], axis=0)
        w2 = jnp.concatenate([qb, augq], axis=0)
        m1 = jnp.full((8, tq), -1e30, F32)
        m2 = m1
        others = iter(range(nk - 1))
        for n in range(nk):
            if n == min(OWN_CHUNK_AT, nk - 1):
                k0 = pl.multiple_of(cd * tk, tk)
                bias = bias_ref[0, 0]
                kc = k_ref[pl.ds(k0, tk), :]
                s1 = _dot(kc, qa) + bias
                s2 = _dot(kc, qb) + bias
                off = 0.0
            else:
                ci = next(others)
                c = ci + (ci >= cd).astype(jnp.int32)
                after = (c > cd).astype(jnp.int32)
                k0 = pl.multiple_of(c * tk, tk)
                lhs = jnp.concatenate([k_ref[pl.ds(k0, tk), :], augk_ref[after]], axis=1)
                s1 = _dot(lhs, w1)
                s2 = _dot(lhs, w2)
                off = c_h * ((k0 - j0) * (1 - 2 * after)).astype(F32)
            slot.s[0, pl.ds(k0, tk), :] = s1
            slot.s[1, pl.ds(k0, tk), :] = s2
            m1 = jnp.maximum(m1, chunk_max(s1) + off)
            m2 = jnp.maximum(m2, chunk_max(s2) + off)
            if n == nk - 1:
                slot.m[0] = jnp.broadcast_to(jnp.max(m1, axis=0, keepdims=True), (8, tq))
                slot.m[1] = jnp.broadcast_to(jnp.max(m2, axis=0, keepdims=True), (8, tq))
            yield

    def pass2_steps(slot, g):
        c_h, j0, cd = tile_scalars(g)
        m1 = slot.m[0][0:1]
        m2 = slot.m[1][0:1]
        acc1 = jnp.zeros((V_ROWS, tq), F32)
        acc2 = acc1
        pending = []

        def value_matmul(acc1, acc2):
            cp, p1, p2 = pending.pop(0)
            vc = vT_ref[:, cp * tk:(cp + 1) * tk]
            return acc1 + _dot(vc, p1), acc2 + _dot(vc, p2)

        for c in range(nk):
            side = jnp.where(c < cd, 1, jnp.where(c > cd, -1, 0))
            off = c_h * (side * (c * tk - j0)).astype(F32)
            p1 = jnp.exp2(slot.s[0, c * tk:(c + 1) * tk, :] - (m1 - off)).astype(BF16)
            p2 = jnp.exp2(slot.s[1, c * tk:(c + 1) * tk, :] - (m2 - off)).astype(BF16)
            pending.append((c, p1, p2))
            if len(pending) > PV_LAG:
                acc1, acc2 = value_matmul(acc1, acc2)
            if c < nk - 1:
                yield
        while pending:
            acc1, acc2 = value_matmul(acc1, acc2)
        lam = (jnp.exp(jnp.sum(lq1_ref[...] * lk1_ref[...], axis=1, keepdims=True))
               - jnp.exp(jnp.sum(lq2_ref[...] * lk2_ref[...], axis=1, keepdims=True))
               + LAMBDA_INIT)
        o1 = acc1[:HEAD_W] / acc1[HEAD_W:HEAD_W + 1]
        o2 = acc2[:HEAD_W] / acc2[HEAD_W:HEAD_W + 1]
        o = o1 - lam * o2
        on = o * lax.rsqrt(jnp.mean(o * o, axis=0, keepdims=True) + RMS_EPS)
        o_ref[...] = (on.T * (gs_ref[...] * (1.0 - LAMBDA_INIT))).astype(BF16)
        yield

    slots = (_Slot(sa_ref, ma_ref), _Slot(sb_ref, mb_ref))
    inner = jnp.logical_and(step > 0, step < n_tiles)

    @pl.when(step == 0)
    def _():
        for _ in pass1_steps(slots[0], step):
            pass

    for parity in range(2):
        @pl.when(jnp.logical_and(inner, step % 2 == parity))
        def _():
            older = pass2_steps(slots[1 - parity], step - 1)
            newer = pass1_steps(slots[parity], step)
            for _ in range(PASS2_LEAD):
                next(older)
            for _ in range(nk - PASS2_LEAD):
                next(older)
                next(newer)
            for _ in range(PASS2_LEAD):
                next(newer)

    @pl.when(step == n_tiles)
    def _():
        for _ in pass2_steps(slots[(n_tiles - 1) % 2], step - 1):
            pass


def _alibi_tables(tk, tq):
    c = (2.0 ** (-8.0 * np.arange(1, N_HEADS + 1) / N_HEADS) * math.log2(math.e)).astype(np.float32)

    def split3(x):
        x1 = x.astype(BF16)
        r = x - x1.astype(np.float32)
        x2 = r.astype(BF16)
        x3 = (r - x2.astype(np.float32)).astype(BF16)
        return x1, x2, x3

    augk = np.zeros((N_HEADS, 2, tk, HEAD_W), BF16)
    augq = np.zeros((N_HEADS, HEAD_W, tq), BF16)
    u = c[:, None] * np.arange(tk, dtype=np.float32)[None, :]
    v = c[:, None] * np.arange(tq, dtype=np.float32)[None, :]
    for n, (un, vn) in enumerate(zip(split3(u), split3(v))):
        augk[:, 0, :, n] = un
        augk[:, 0, :, 3 + n] = 1.0
        augq[:, n, :] = 1.0
        augq[:, 3 + n, :] = -vn
    augk[:, 1] = -augk[:, 0]
    pos = np.arange(tk // tq, dtype=np.float32)[:, None, None] * tq
    dist = np.abs(np.arange(tk, dtype=np.float32)[None, :, None]
                  - np.arange(tq, dtype=np.float32)[None, None, :] - pos)
    bias = (-c[:, None, None, None] * dist[None]).astype(np.float32)
    return c, augk.reshape(2 * N_HEADS, tk, HEAD_W), augq, bias


def _tail_kernel(x_ref, yf_ref, ya_ref, p_ref, gmix_ref, wbg_ref, bbg_ref, wao_ref, wo_ref,
                 gffn_ref, wg_ref, wu_ref, wd_ref, gple_ref, wpg_ref, wpp_ref, gfin_ref, out_ref):
    d = x_ref.shape[1]
    x = x_ref[...]
    h = _rms(x, gmix_ref[...]).astype(BF16)
    gates = jax.nn.sigmoid(_dot(h, wbg_ref[...]) + bbg_ref[...])
    ya = _dot(ya_ref[...], wao_ref[...])
    merged = gates[:, :d] * yf_ref[...] + gates[:, d:] * ya
    x = x + _dot(merged.astype(BF16), wo_ref[...])

    h2 = _rms(x, gffn_ref[...]).astype(BF16)
    gg = _dot(h2, wg_ref[...])
    uu = _dot(h2, wu_ref[...])
    act = (gg * jax.nn.sigmoid(gg)) * uu
    x = x + _dot(act.astype(BF16), wd_ref[...])

    h3 = _rms(x, gple_ref[...]).astype(BF16)
    pg = jax.nn.sigmoid(_dot(h3, wpg_ref[...]))
    x = x + pg * _dot(p_ref[...].astype(BF16), wpp_ref[...])
    out_ref[...] = _rms(x, gfin_ref[...])


def _const_spec(shape):
    return pl.BlockSpec(shape, lambda *_: (0,) * len(shape), pipeline_mode=pl.Buffered(1))


def _dft_tables(s_len):
    n1 = math.isqrt(s_len)
    n2 = s_len // n1
    assert n1 * n2 == s_len and n1 % 16 == 0 and n2 % 16 == 0

    def cos_sin(m, size):
        ang = 2.0 * np.pi * (m % size) / size
        return np.cos(ang), np.sin(ang)

    c1, s1 = cos_sin(np.outer(np.arange(n1), np.arange(n1)), n1)
    f1 = (np.concatenate([c1, -s1], axis=0) * n1 ** -0.5).astype(np.float32)
    ctw, stw = cos_sin(np.outer(np.arange(n1), np.arange(n2)), s_len)
    r = min(FFT1_ROWS, n2)

    def per_step(tw):
        rows = tw.reshape(n1, n2 // r, r).transpose(1, 0, 2).reshape(n2 // r, n1 * r, 1)
        return np.broadcast_to(rows, (n2 // r, n1 * r, GROUP_DIM)).astype(np.float32)

    twr, twi = per_step(ctw), per_step(-stw)
    c2, s2 = cos_sin(np.outer(np.arange(n2), np.arange(n2)), n2)
    f2 = (np.block([[c2, s2], [-s2, c2]]) * n2 ** -0.5).astype(np.float32)
    cc, sc = cos_sin(np.outer(np.arange(GROUP_DIM), np.arange(GROUP_DIM)), GROUP_DIM)
    cs = (np.concatenate([cc, sc], axis=0) * GROUP_DIM ** -0.5).astype(np.float32)
    return n1, n2, f1, twr, twi, f2, cs


def kernel(x, p, g_mix, w_in, lambda_q1, lambda_k1, lambda_q2, lambda_k2, g_subln,
           w_fourier_out, w_attn_out, w_branch_gate, b_branch_gate, w_o, g_ffn, w_ffn_gate,
           w_ffn_up, w_ffn_down, g_ple, w_ple_gate, w_ple_proj, g_final):
    bsz, s_len, d = x.shape
    t_tok = bsz * s_len
    d_ff = w_ffn_gate.shape[-1]
    ple = p.shape[-1]
    assert w_in.shape[0] == 1 and d == N_HEADS * HEAD_W
    x2 = x.reshape(t_tok, d)
    p2 = p[0].reshape(t_tok, ple)
    n1, n2, f1, twr, twi, f2, cs = _dft_tables(s_len)
    cparams = functools.partial(pltpu.CompilerParams, vmem_limit_bytes=VMEM_LIMIT)

    tm = min(512, s_len)
    row = lambda i: (i, 0)
    col = lambda i: (0, i)
    u_f, qT, k_arr, vT = pl.pallas_call(
        _in_proj_kernel,
        grid=(t_tok // tm,),
        in_specs=[pl.BlockSpec((tm, d), row), _const_spec((1, d)), _const_spec((d, 4 * d))],
        out_specs=[pl.BlockSpec((tm, d), row), pl.BlockSpec((d, tm), col),
                   pl.BlockSpec((tm, d), row), pl.BlockSpec((N_HEADS * V_ROWS, tm), col)],
        out_shape=[jax.ShapeDtypeStruct((t_tok, d), F32), jax.ShapeDtypeStruct((d, t_tok), BF16),
                   jax.ShapeDtypeStruct((t_tok, d), BF16),
                   jax.ShapeDtypeStruct((N_HEADS * V_ROWS, t_tok), BF16)],
        compiler_params=cparams(dimension_semantics=("arbitrary",)),
        name="in_proj",
    )(x2, g_mix[0][None, :], w_in[0].astype(BF16))

    r = min(FFT1_ROWS, n2)
    blk = pl.BlockSpec((n1, r, d), lambda b, j: (b, j, 0))
    tw_spec = pl.BlockSpec((1, n1 * r, GROUP_DIM), lambda b, j: (j, 0, 0))
    f1k = jnp.kron(jnp.asarray(f1), jnp.eye(r, dtype=F32)).astype(BF16)
    t_re, t_im = pl.pallas_call(
        _fft1_kernel,
        grid=(bsz, n2 // r),
        in_specs=[blk, _const_spec((2 * n1 * r, n1 * r)), tw_spec, tw_spec],
        out_specs=[blk, blk],
        out_shape=[jax.ShapeDtypeStruct((bsz * n1, n2, d), BF16)] * 2,
        compiler_params=cparams(dimension_semantics=("arbitrary", "arbitrary")),
        name="fft_stage1",
    )(u_f.reshape(bsz * n1, n2, d), f1k, jnp.asarray(twr), jnp.asarray(twi))

    grp = min(FFT2_GROUP, n1)
    tblk = pl.BlockSpec((grp * n2, d), row)
    y_f = pl.pallas_call(
        functools.partial(_fft2_kernel, n2=n2),
        grid=(bsz * n1 // grp,),
        in_specs=[tblk, tblk, _const_spec((2 * n2, 2 * n2)),
                  _const_spec((2 * GROUP_DIM, GROUP_DIM)), _const_spec((d, d))],
        out_specs=pl.BlockSpec((n2, grp * d), lambda s: (s // (n1 // grp), s % (n1 // grp))),
        out_shape=jax.ShapeDtypeStruct((bsz * n2, n1 * d), F32),
        compiler_params=cparams(dimension_semantics=("arbitrary",)),
        name="fft_stage2",
    )(t_re.reshape(t_tok, d), t_im.reshape(t_tok, d), jnp.asarray(f2).astype(BF16),
      jnp.asarray(cs).astype(BF16),
      w_fourier_out[0].astype(BF16))
    y_f = y_f.reshape(t_tok, d)

    tq = min(256, s_len)
    tk = min(512, s_len)
    nq = s_len // tq
    cvec, augk, augq, bias = _alibi_tables(tk, tq)
    n_tiles = bsz * N_HEADS * nq

    def tile(g):
        gh, t = _divmod_pow2(g, nq)
        return _divmod_pow2(gh, N_HEADS) + (t,)

    cur = lambda g: tile(jnp.minimum(g, n_tiles - 1))
    prev = lambda g: tile(jnp.maximum(g - 1, 0))
    lam_spec = pl.BlockSpec((1, HEAD_DIM), lambda g: (0, 0))
    y_a = pl.pallas_call(
        functools.partial(_attn_kernel, tk=tk, nq=nq, n_tiles=n_tiles),
        grid=(n_tiles + 1,),
        in_specs=[pl.BlockSpec(memory_space=pltpu.SMEM),
                  pl.BlockSpec((HEAD_W, tq), lambda g: (cur(g)[1], cur(g)[0] * nq + cur(g)[2])),
                  pl.BlockSpec((s_len, HEAD_W), lambda g: (cur(g)[0], cur(g)[1])),
                  pl.BlockSpec((V_ROWS, s_len), lambda g: (prev(g)[1], prev(g)[0])),
                  pl.BlockSpec((2, tk, HEAD_W), lambda g: (cur(g)[1], 0, 0)),
                  pl.BlockSpec((1, HEAD_W, tq), lambda g: (cur(g)[1], 0, 0)),
                  pl.BlockSpec((1, 1, tk, tq), lambda g: (cur(g)[1], _divmod_pow2(cur(g)[2], tk // tq)[1], 0, 0)),
                  lam_spec, lam_spec, lam_spec, lam_spec,
                  pl.BlockSpec((1, HEAD_W), lambda g: (0, 0))],
        out_specs=pl.BlockSpec((tq, HEAD_W), lambda g: (prev(g)[0] * nq + prev(g)[2], prev(g)[1])),
        out_shape=jax.ShapeDtypeStruct((t_tok, d), BF16),
        scratch_shapes=[pltpu.VMEM((2, s_len, tq), F32), pltpu.VMEM((2, s_len, tq), F32),
                        pltpu.VMEM((2, 8, tq), F32), pltpu.VMEM((2, 8, tq), F32)],
        compiler_params=cparams(dimension_semantics=("arbitrary",)),
        name="diff_attn",
    )(jnp.asarray(cvec), qT, k_arr, vT, jnp.asarray(augk), jnp.asarray(augq), jnp.asarray(bias),
      lambda_q1, lambda_k1, lambda_q2, lambda_k2, g_subln)

    tt = min(512, s_len)
    vec = lambda a: a.reshape(1, -1)
    out = pl.pallas_call(
        _tail_kernel,
        grid=(t_tok // tt,),
        in_specs=[pl.BlockSpec((tt, d), row), pl.BlockSpec((tt, d), row),
                  pl.BlockSpec((tt, d), row), pl.BlockSpec((tt, ple), row),
                  _const_spec((1, d)), _const_spec((d, 2 * d)), _const_spec((1, 2 * d)),
                  _const_spec((d, d)), _const_spec((d, d)),
                  _const_spec((1, d)), _const_spec((d, d_ff)), _const_spec((d, d_ff)),
                  _const_spec((d_ff, d)),
                  _const_spec((1, d)), _const_spec((d, d)), _const_spec((ple, d)),
                  _const_spec((1, d))],
        out_specs=pl.BlockSpec((tt, d), row),
        out_shape=jax.ShapeDtypeStruct((t_tok, d), F32),
        compiler_params=cparams(dimension_semantics=("arbitrary",)),
        name="tail",
    )(x2, y_f, y_a, p2, vec(g_mix[0]), w_branch_gate[0].astype(BF16), vec(b_branch_gate[0]),
      w_attn_out[0].astype(BF16), w_o[0].astype(BF16), vec(g_ffn[0]),
      w_ffn_gate[0].astype(BF16), w_ffn_up[0].astype(BF16), w_ffn_down[0].astype(BF16),
      vec(g_ple[0]), w_ple_gate[0].astype(BF16), w_ple_proj[0].astype(BF16), vec(g_final))
    return out.reshape(bsz, s_len, d)
```

```python
import functools
import itertools
import math
from typing import Any, NamedTuple

import jax
import jax.numpy as jnp
import numpy as np
from jax import lax
from jax.experimental import pallas as pl
from jax.experimental.pallas import tpu as pltpu

F32 = jnp.float32
BF16 = jnp.bfloat16

N_HEADS = 8
HEAD_DIM = 64
HEAD_W = 2 * HEAD_DIM
GROUP_DIM = 128
RMS_EPS = 1e-6
LAMBDA_INIT = 0.8 - 0.6 * math.exp(-0.3 * 0)
V_ROWS = HEAD_W + 16
VMEM_LIMIT = 56 * 1024 * 1024
FFT1_ROWS = 16
FFT2_GROUP = 8
TILES_PER_STEP = 2
PV_LAG = 1
PASS2_LEAD = 1
OWN_CHUNK_AT = 3
Q_SCALE = HEAD_DIM ** -0.5 * math.log2(math.e)


def _rms(x, g):
    return x * lax.rsqrt(jnp.mean(x * x, axis=-1, keepdims=True) + RMS_EPS) * g


def _dot(a, b):
    return jnp.dot(a, b, preferred_element_type=F32)


def _divmod_pow2(x, n):
    assert n > 0 and n & (n - 1) == 0, n
    return x >> (n.bit_length() - 1), x & (n - 1)


def _in_proj_kernel(x_ref, g_ref, w_ref, uf_ref, qT_ref, k_ref, vT_ref):
    d = x_ref.shape[1]
    tm = x_ref.shape[0]
    h = _rms(x_ref[...], g_ref[...]).astype(BF16)
    uf_ref[...] = _dot(h, w_ref[:, 0:d])
    q = _dot(h, w_ref[:, d:2 * d]) * Q_SCALE
    qT_ref[...] = q.T.astype(BF16)
    k_ref[...] = _dot(h, w_ref[:, 2 * d:3 * d]).astype(BF16)
    vT = _dot(h, w_ref[:, 3 * d:4 * d]).T
    ones = jnp.ones((V_ROWS - HEAD_W, tm), BF16)
    for hh in range(N_HEADS):
        vT_ref[hh * V_ROWS:hh * V_ROWS + HEAD_W, :] = vT[hh * HEAD_W:(hh + 1) * HEAD_W, :].astype(BF16)
        vT_ref[hh * V_ROWS + HEAD_W:(hh + 1) * V_ROWS, :] = ones


def _fft1_kernel(x_ref, f1k_ref, twr_ref, twi_ref, tre_ref, tim_ref):
    n1, r, d = x_ref.shape
    x = x_ref[...].reshape(n1 * r, d).astype(BF16)
    t = _dot(f1k_ref[...], x)
    wr = twr_ref[0]
    wi = twi_ref[0]
    for cc in range(d // GROUP_DIM):
        lo, hi = cc * GROUP_DIM, (cc + 1) * GROUP_DIM
        a = t[:n1 * r, lo:hi]
        b = t[n1 * r:, lo:hi]
        tre_ref[:, :, lo:hi] = (a * wr - b * wi).astype(BF16).reshape(n1, r, GROUP_DIM)
        tim_ref[:, :, lo:hi] = (a * wi + b * wr).astype(BF16).reshape(n1, r, GROUP_DIM)


def _fft2_kernel(tre_ref, tim_ref, f2_ref, cs_ref, wfo_ref, y_ref, *, n2):
    d = tre_ref.shape[1]
    grp = tre_ref.shape[0] // n2
    f2 = f2_ref[...]
    zs = []
    for i in range(grp):
        rhs = jnp.concatenate([tre_ref[i * n2:(i + 1) * n2, :], tim_ref[i * n2:(i + 1) * n2, :]], axis=0)
        zs.append(_dot(f2, rhs))
    re = jnp.concatenate([z[:n2] for z in zs], axis=0).astype(BF16)
    im = jnp.concatenate([z[n2:] for z in zs], axis=0).astype(BF16)
    cs = cs_ref[...]
    ys = []
    for g in range(d // GROUP_DIM):
        lo, hi = g * GROUP_DIM, (g + 1) * GROUP_DIM
        ys.append(_dot(jnp.concatenate([re[:, lo:hi], im[:, lo:hi]], axis=1), cs))
    y = jnp.concatenate(ys, axis=1).astype(BF16)
    out = _dot(y, wfo_ref[...])
    for i in range(grp):
        y_ref[:, i * d:(i + 1) * d] = out[i * n2:(i + 1) * n2, :]


class _Slot(NamedTuple):
    s: Any
    m: Any


def _attn_kernel(cvec_ref, qT_ref, k_ref, vT_ref, augk_ref, augq_ref, bias_ref, lq1_ref, lk1_ref, lq2_ref,
                 lk2_ref, gs_ref, o_ref, *scratch, tk, tq, nq, n_tiles):
    step = pl.program_id(0)
    per = TILES_PER_STEP
    s_len = k_ref.shape[0]
    nk = s_len // tk

    def tile_scalars(g):
        gh, t = _divmod_pow2(g, nq)
        j0 = t * tq
        return cvec_ref[_divmod_pow2(gh, N_HEADS)[1]], j0, _divmod_pow2(j0, tk)[0]

    def chunk_max(s):
        return jnp.max(s.reshape(tk // 8, 8, tq), axis=0)

    def pass1_steps(slot, g, i):
        c_h, j0, cd = tile_scalars(g)
        qT = qT_ref[:, i * tq:(i + 1) * tq]
        zq = jnp.zeros((HEAD_DIM, tq), BF16)
        qa = jnp.concatenate([qT[:HEAD_DIM], zq], axis=0)
        qb = jnp.concatenate([zq, qT[HEAD_DIM:]], axis=0)
        augq = augq_ref[0]
        w1 = jnp.concatenate([qa, augq], axis=0)
        w2 = jnp.concatenate([qb, augq], axis=0)
        m1 = jnp.full((8, tq), -1e30, F32)
        m2 = m1
        others = iter(range(nk - 1))
        for n in range(nk):
            if n == min(OWN_CHUNK_AT, nk - 1):
                k0 = pl.multiple_of(cd * tk, tk)
                bias = bias_ref[0, _divmod_pow2(_divmod_pow2(j0, tq)[0], tk // tq)[1]]
                kc = k_ref[pl.ds(k0, tk), :]
                s1 = _dot(kc, qa) + bias
                s2 = _dot(kc, qb) + bias
                off = 0.0
            else:
                ci = next(others)
                c = ci + (ci >= cd).astype(jnp.int32)
                after = (c > cd).astype(jnp.int32)
                k0 = pl.multiple_of(c * tk, tk)
                lhs = jnp.concatenate([k_ref[pl.ds(k0, tk), :], augk_ref[after]], axis=1)
                s1 = _dot(lhs, w1)
                s2 = _dot(lhs, w2)
                off = c_h * ((k0 - j0) * (1 - 2 * after)).astype(F32)
            slot.s[0, pl.ds(k0, tk), :] = s1
            slot.s[1, pl.ds(k0, tk), :] = s2
            m1 = jnp.maximum(m1, chunk_max(s1) + off)
            m2 = jnp.maximum(m2, chunk_max(s2) + off)
            if n == nk - 1:
                slot.m[0] = jnp.broadcast_to(jnp.max(m1, axis=0, keepdims=True), (8, tq))
                slot.m[1] = jnp.broadcast_to(jnp.max(m2, axis=0, keepdims=True), (8, tq))
            yield

    def pass2_steps(slot, g, i):
        c_h, j0, cd = tile_scalars(g)
        m1 = slot.m[0][0:1]
        m2 = slot.m[1][0:1]
        acc1 = jnp.zeros((V_ROWS, tq), F32)
        acc2 = acc1
        pending = []

        def value_matmul(acc1, acc2):
            cp, p1, p2 = pending.pop(0)
            vc = vT_ref[:, cp * tk:(cp + 1) * tk]
            return acc1 + _dot(vc, p1), acc2 + _dot(vc, p2)

        for c in range(nk):
            side = jnp.where(c < cd, 1, jnp.where(c > cd, -1, 0))
            off = c_h * (side * (c * tk - j0)).astype(F32)
            p1 = jnp.exp2(slot.s[0, c * tk:(c + 1) * tk, :] - (m1 - off)).astype(BF16)
            p2 = jnp.exp2(slot.s[1, c * tk:(c + 1) * tk, :] - (m2 - off)).astype(BF16)
            pending.append((c, p1, p2))
            if len(pending) > PV_LAG:
                acc1, acc2 = value_matmul(acc1, acc2)
            if c < nk - 1:
                yield
        while pending:
            acc1, acc2 = value_matmul(acc1, acc2)
        lam = (jnp.exp(jnp.sum(lq1_ref[...] * lk1_ref[...], axis=1, keepdims=True))
               - jnp.exp(jnp.sum(lq2_ref[...] * lk2_ref[...], axis=1, keepdims=True))
               + LAMBDA_INIT)
        o1 = acc1[:HEAD_W] / acc1[HEAD_W:HEAD_W + 1]
        o2 = acc2[:HEAD_W] / acc2[HEAD_W:HEAD_W + 1]
        o = o1 - lam * o2
        on = o * lax.rsqrt(jnp.mean(o * o, axis=0, keepdims=True) + RMS_EPS)
        o_ref[i * tq:(i + 1) * tq, :] = (on.T * (gs_ref[...] * (1.0 - LAMBDA_INIT))).astype(BF16)
        yield

    sets = [[_Slot(scratch[p * per + i], scratch[(2 + p) * per + i]) for i in range(per)]
            for p in range(2)]
    n_groups = n_tiles // per

    def pass1_group(p, u):
        return itertools.chain(*[pass1_steps(sets[p][i], u * per + i, i) for i in range(per)])

    def pass2_group(p, u):
        return itertools.chain(*[pass2_steps(sets[p][i], u * per + i, i) for i in range(per)])

    inner = jnp.logical_and(step > 0, step < n_groups)

    @pl.when(step == 0)
    def _():
        for _ in pass1_group(0, step):
            pass

    for parity in range(2):
        @pl.when(jnp.logical_and(inner, (step & 1) == parity))
        def _():
            older = pass2_group(1 - parity, step - 1)
            newer = pass1_group(parity, step)
            for _ in range(PASS2_LEAD):
                next(older)
            for _ in range(per * nk - PASS2_LEAD):
                next(older)
                next(newer)
            for _ in range(PASS2_LEAD):
                next(newer)

    @pl.when(step == n_groups)
    def _():
        for _ in pass2_group((n_groups - 1) % 2, step - 1):
            pass


def _alibi_tables(tk, tq):
    c = (2.0 ** (-8.0 * np.arange(1, N_HEADS + 1) / N_HEADS) * math.log2(math.e)).astype(np.float32)

    def split3(x):
        x1 = x.astype(BF16)
        r = x - x1.astype(np.float32)
        x2 = r.astype(BF16)
        x3 = (r - x2.astype(np.float32)).astype(BF16)
        return x1, x2, x3

    augk = np.zeros((N_HEADS, 2, tk, HEAD_W), BF16)
    augq = np.zeros((N_HEADS, HEAD_W, tq), BF16)
    u = c[:, None] * np.arange(tk, dtype=np.float32)[None, :]
    v = c[:, None] * np.arange(tq, dtype=np.float32)[None, :]
    for n, (un, vn) in enumerate(zip(split3(u), split3(v))):
        augk[:, 0, :, n] = un
        augk[:, 0, :, 3 + n] = 1.0
        augq[:, n, :] = 1.0
        augq[:, 3 + n, :] = -vn
    augk[:, 1] = -augk[:, 0]
    pos = np.arange(tk // tq, dtype=np.float32)[:, None, None] * tq
    dist = np.abs(np.arange(tk, dtype=np.float32)[None, :, None]
                  - np.arange(tq, dtype=np.float32)[None, None, :] - pos)
    bias = (-c[:, None, None, None] * dist[None]).astype(np.float32)
    return c, augk.reshape(2 * N_HEADS, tk, HEAD_W), augq, bias


def _tail_kernel(x_ref, yf_ref, ya_ref, p_ref, gmix_ref, wbg_ref, bbg_ref, wao_ref, wo_ref,
                 gffn_ref, wg_ref, wu_ref, wd_ref, gple_ref, wpg_ref, wpp_ref, gfin_ref, out_ref):
    d = x_ref.shape[1]
    x = x_ref[...]
    h = _rms(x, gmix_ref[...]).astype(BF16)
    gates = jax.nn.sigmoid(_dot(h, wbg_ref[...]) + bbg_ref[...])
    ya = _dot(ya_ref[...], wao_ref[...])
    merged = gates[:, :d] * yf_ref[...] + gates[:, d:] * ya
    x = x + _dot(merged.astype(BF16), wo_ref[...])

    h2 = _rms(x, gffn_ref[...]).astype(BF16)
    gg = _dot(h2, wg_ref[...])
    uu = _dot(h2, wu_ref[...])
    act = (gg * jax.nn.sigmoid(gg)) * uu
    x = x + _dot(act.astype(BF16), wd_ref[...])

    h3 = _rms(x, gple_ref[...]).astype(BF16)
    pg = jax.nn.sigmoid(_dot(h3, wpg_ref[...]))
    x = x + pg * _dot(p_ref[...].astype(BF16), wpp_ref[...])
    out_ref[...] = _rms(x, gfin_ref[...])


def _const_spec(shape):
    return pl.BlockSpec(shape, lambda *_: (0,) * len(shape), pipeline_mode=pl.Buffered(1))


def _dft_tables(s_len):
    n1 = math.isqrt(s_len)
    n2 = s_len // n1
    assert n1 * n2 == s_len and n1 % 16 == 0 and n2 % 16 == 0

    def cos_sin(m, size):
        ang = 2.0 * np.pi * (m % size) / size
        return np.cos(ang), np.sin(ang)

    c1, s1 = cos_sin(np.outer(np.arange(n1), np.arange(n1)), n1)
    f1 = (np.concatenate([c1, -s1], axis=0) * n1 ** -0.5).astype(np.float32)
    ctw, stw = cos_sin(np.outer(np.arange(n1), np.arange(n2)), s_len)
    r = min(FFT1_ROWS, n2)

    def per_step(tw):
        rows = tw.reshape(n1, n2 // r, r).transpose(1, 0, 2).reshape(n2 // r, n1 * r, 1)
        return np.broadcast_to(rows, (n2 // r, n1 * r, GROUP_DIM)).astype(np.float32)

    twr, twi = per_step(ctw), per_step(-stw)
    c2, s2 = cos_sin(np.outer(np.arange(n2), np.arange(n2)), n2)
    f2 = (np.block([[c2, s2], [-s2, c2]]) * n2 ** -0.5).astype(np.float32)
    cc, sc = cos_sin(np.outer(np.arange(GROUP_DIM), np.arange(GROUP_DIM)), GROUP_DIM)
    cs = (np.concatenate([cc, sc], axis=0) * GROUP_DIM ** -0.5).astype(np.float32)
    return n1, n2, f1, twr, twi, f2, cs


def kernel(x, p, g_mix, w_in, lambda_q1, lambda_k1, lambda_q2, lambda_k2, g_subln,
           w_fourier_out, w_attn_out, w_branch_gate, b_branch_gate, w_o, g_ffn, w_ffn_gate,
           w_ffn_up, w_ffn_down, g_ple, w_ple_gate, w_ple_proj, g_final):
    bsz, s_len, d = x.shape
    t_tok = bsz * s_len
    d_ff = w_ffn_gate.shape[-1]
    ple = p.shape[-1]
    assert w_in.shape[0] == 1 and d == N_HEADS * HEAD_W
    x2 = x.reshape(t_tok, d)
    p2 = p[0].reshape(t_tok, ple)
    n1, n2, f1, twr, twi, f2, cs = _dft_tables(s_len)
    cparams = functools.partial(pltpu.CompilerParams, vmem_limit_bytes=VMEM_LIMIT)

    tm = min(512, s_len)
    row = lambda i: (i, 0)
    col = lambda i: (0, i)
    u_f, qT, k_arr, vT = pl.pallas_call(
        _in_proj_kernel,
        grid=(t_tok // tm,),
        in_specs=[pl.BlockSpec((tm, d), row), _const_spec((1, d)), _const_spec((d, 4 * d))],
        out_specs=[pl.BlockSpec((tm, d), row), pl.BlockSpec((d, tm), col),
                   pl.BlockSpec((tm, d), row), pl.BlockSpec((N_HEADS * V_ROWS, tm), col)],
        out_shape=[jax.ShapeDtypeStruct((t_tok, d), F32), jax.ShapeDtypeStruct((d, t_tok), BF16),
                   jax.ShapeDtypeStruct((t_tok, d), BF16),
                   jax.ShapeDtypeStruct((N_HEADS * V_ROWS, t_tok), BF16)],
        compiler_params=cparams(dimension_semantics=("arbitrary",)),
        name="in_proj",
    )(x2, g_mix[0][None, :], w_in[0].astype(BF16))

    r = min(FFT1_ROWS, n2)
    blk = pl.BlockSpec((n1, r, d), lambda b, j: (b, j, 0))
    tw_spec = pl.BlockSpec((1, n1 * r, GROUP_DIM), lambda b, j: (j, 0, 0))
    f1k = jnp.asarray(np.kron(f1, np.eye(r, dtype=np.float32))).astype(BF16)
    t_re, t_im = pl.pallas_call(
        _fft1_kernel,
        grid=(bsz, n2 // r),
        in_specs=[blk, _const_spec((2 * n1 * r, n1 * r)), tw_spec, tw_spec],
        out_specs=[blk, blk],
        out_shape=[jax.ShapeDtypeStruct((bsz * n1, n2, d), BF16)] * 2,
        compiler_params=cparams(dimension_semantics=("arbitrary", "arbitrary")),
        name="fft_stage1",
    )(u_f.reshape(bsz * n1, n2, d), f1k, jnp.asarray(twr), jnp.asarray(twi))

    grp = min(FFT2_GROUP, n1)
    tblk = pl.BlockSpec((grp * n2, d), row)
    y_f = pl.pallas_call(
        functools.partial(_fft2_kernel, n2=n2),
        grid=(bsz * n1 // grp,),
        in_specs=[tblk, tblk, _const_spec((2 * n2, 2 * n2)),
                  _const_spec((2 * GROUP_DIM, GROUP_DIM)), _const_spec((d, d))],
        out_specs=pl.BlockSpec((n2, grp * d), lambda s: (s // (n1 // grp), s % (n1 // grp))),
        out_shape=jax.ShapeDtypeStruct((bsz * n2, n1 * d), F32),
        compiler_params=cparams(dimension_semantics=("arbitrary",)),
        name="fft_stage2",
    )(t_re.reshape(t_tok, d), t_im.reshape(t_tok, d), jnp.asarray(f2).astype(BF16),
      jnp.asarray(cs).astype(BF16),
      w_fourier_out[0].astype(BF16))
    y_f = y_f.reshape(t_tok, d)

    tq = min(256, s_len)
    tk = min(512, s_len)
    nq = s_len // tq
    cvec, augk, augq, bias = _alibi_tables(tk, tq)
    n_tiles = bsz * N_HEADS * nq

    per = TILES_PER_STEP
    npq = nq // per
    n_groups = n_tiles // per
    assert nq % per == 0

    def group(u):
        uh, tg = _divmod_pow2(u, npq)
        return _divmod_pow2(uh, N_HEADS) + (tg,)

    cur = lambda u: group(jnp.minimum(u, n_groups - 1))
    prev = lambda u: group(jnp.maximum(u - 1, 0))
    lam_spec = pl.BlockSpec((1, HEAD_DIM), lambda u: (0, 0))
    y_a = pl.pallas_call(
        functools.partial(_attn_kernel, tk=tk, tq=tq, nq=nq, n_tiles=n_tiles),
        grid=(n_groups + 1,),
        in_specs=[pl.BlockSpec(memory_space=pltpu.SMEM),
                  pl.BlockSpec((HEAD_W, per * tq), lambda u: (cur(u)[1], cur(u)[0] * npq + cur(u)[2])),
                  pl.BlockSpec((s_len, HEAD_W), lambda u: (cur(u)[0], cur(u)[1])),
                  pl.BlockSpec((V_ROWS, s_len), lambda u: (prev(u)[1], prev(u)[0])),
                  pl.BlockSpec((2, tk, HEAD_W), lambda u: (cur(u)[1], 0, 0)),
                  pl.BlockSpec((1, HEAD_W, tq), lambda u: (cur(u)[1], 0, 0)),
                  pl.BlockSpec((1, tk // tq, tk, tq), lambda u: (cur(u)[1], 0, 0, 0)),
                  lam_spec, lam_spec, lam_spec, lam_spec,
                  pl.BlockSpec((1, HEAD_W), lambda u: (0, 0))],
        out_specs=pl.BlockSpec((per * tq, HEAD_W),
                               lambda u: (prev(u)[0] * npq + prev(u)[2], prev(u)[1])),
        out_shape=jax.ShapeDtypeStruct((t_tok, d), BF16),
        scratch_shapes=([pltpu.VMEM((2, s_len, tq), F32)] * (2 * per)
                        + [pltpu.VMEM((2, 8, tq), F32)] * (2 * per)),
        compiler_params=cparams(dimension_semantics=("arbitrary",)),
        name="diff_attn",
    )(jnp.asarray(cvec), qT, k_arr, vT, jnp.asarray(augk), jnp.asarray(augq), jnp.asarray(bias),
      lambda_q1, lambda_k1, lambda_q2, lambda_k2, g_subln)

    tt = min(512, s_len)
    vec = lambda a: a.reshape(1, -1)
    out = pl.pallas_call(
        _tail_kernel,
        grid=(t_tok // tt,),
        in_specs=[pl.BlockSpec((tt, d), row), pl.BlockSpec((tt, d), row),
                  pl.BlockSpec((tt, d), row), pl.BlockSpec((tt, ple), row),
                  _const_spec((1, d)), _const_spec((d, 2 * d)), _const_spec((1, 2 * d)),
                  _const_spec((d, d)), _const_spec((d, d)),
                  _const_spec((1, d)), _const_spec((d, d_ff)), _const_spec((d, d_ff)),
                  _const_spec((d_ff, d)),
                  _const_spec((1, d)), _const_spec((d, d)), _const_spec((ple, d)),
                  _const_spec((1, d))],
        out_specs=pl.BlockSpec((tt, d), row),
        out_shape=jax.ShapeDtypeStruct((t_tok, d), F32),
        compiler_params=cparams(dimension_semantics=("arbitrary",)),
        name="tail",
    )(x2, y_f, y_a, p2, vec(g_mix[0]), w_branch_gate[0].astype(BF16), vec(b_branch_gate[0]),
      w_attn_out[0].astype(BF16), w_o[0].astype(BF16), vec(g_ffn[0]),
      w_ffn_gate[0].astype(BF16), w_ffn_up[0].astype(BF16), w_ffn_down[0].astype(BF16),
      vec(g_ple[0]), w_ple_gate[0].astype(BF16), w_ple_proj[0].astype(BF16), vec(g_final))
    return out.reshape(bsz, s_len, d)
```

```python
import functools
import itertools
import math
from typing import Any, NamedTuple

import jax
import jax.numpy as jnp
import numpy as np
from jax import lax
from jax.experimental import pallas as pl
from jax.experimental.pallas import tpu as pltpu

F32 = jnp.float32
BF16 = jnp.bfloat16

N_HEADS = 8
HEAD_DIM = 64
HEAD_W = 2 * HEAD_DIM
GROUP_DIM = 128
RMS_EPS = 1e-6
LAMBDA_INIT = 0.8 - 0.6 * math.exp(-0.3 * 0)
V_ROWS = HEAD_W + 16
VMEM_LIMIT = 56 * 1024 * 1024
FFT1_ROWS = 16
FFT2_GROUP = 8
TILES_PER_STEP = 2
PV_LAG = 1
PASS2_LEAD = 1
OWN_CHUNK_AT = 3
Q_SCALE = HEAD_DIM ** -0.5 * math.log2(math.e)


def _rms(x, g):
    return x * lax.rsqrt(jnp.mean(x * x, axis=-1, keepdims=True) + RMS_EPS) * g


def _dot(a, b):
    return jnp.dot(a, b, preferred_element_type=F32)


def _divmod_pow2(x, n):
    assert n > 0 and n & (n - 1) == 0, n
    return x >> (n.bit_length() - 1), x & (n - 1)


def _in_proj_kernel(x_ref, g_ref, w_ref, uf_ref, qT_ref, k_ref, vT_ref):
    d = x_ref.shape[1]
    tm = x_ref.shape[0]
    h = _rms(x_ref[...], g_ref[...]).astype(BF16)
    uf_ref[...] = _dot(h, w_ref[:, 0:d])
    q = _dot(h, w_ref[:, d:2 * d]) * Q_SCALE
    qT_ref[...] = q.T.astype(BF16)
    k_ref[...] = _dot(h, w_ref[:, 2 * d:3 * d]).astype(BF16)
    vT = _dot(h, w_ref[:, 3 * d:4 * d]).T
    ones = jnp.ones((V_ROWS - HEAD_W, tm), BF16)
    for hh in range(N_HEADS):
        vT_ref[hh * V_ROWS:hh * V_ROWS + HEAD_W, :] = vT[hh * HEAD_W:(hh + 1) * HEAD_W, :].astype(BF16)
        vT_ref[hh * V_ROWS + HEAD_W:(hh + 1) * V_ROWS, :] = ones


def _fft1_kernel(x_ref, f1_ref, tre_ref, tim_ref):
    n1, r, d = x_ref.shape
    xs = jnp.swapaxes(x_ref[...], 0, 1)
    f1 = f1_ref[...]
    ts = [_dot(f1, xs[q].astype(BF16)) for q in range(r)]
    tre_ref[...] = jnp.swapaxes(jnp.stack([t[:n1] for t in ts], axis=0), 0, 1).astype(BF16)
    tim_ref[...] = jnp.swapaxes(jnp.stack([t[n1:] for t in ts], axis=0), 0, 1).astype(BF16)


def _fft2_kernel(tre_ref, tim_ref, f2_ref, cs_ref, wfo_ref, y_ref, *, n2):
    d = tre_ref.shape[1]
    grp = tre_ref.shape[0] // n2
    zs = []
    for i in range(grp):
        rhs = jnp.concatenate([tre_ref[i * n2:(i + 1) * n2, :], tim_ref[i * n2:(i + 1) * n2, :]], axis=0)
        zs.append(_dot(f2_ref[i], rhs))
    re = jnp.concatenate([z[:n2] for z in zs], axis=0).astype(BF16)
    im = jnp.concatenate([z[n2:] for z in zs], axis=0).astype(BF16)
    cs = cs_ref[...]
    ys = []
    for g in range(d // GROUP_DIM):
        lo, hi = g * GROUP_DIM, (g + 1) * GROUP_DIM
        ys.append(_dot(jnp.concatenate([re[:, lo:hi], im[:, lo:hi]], axis=1), cs))
    y = jnp.concatenate(ys, axis=1).astype(BF16)
    out = _dot(y, wfo_ref[...])
    y_ref[...] = jnp.swapaxes(out.reshape(grp, n2, d), 0, 1)


class _Slot(NamedTuple):
    s: Any
    m: Any


def _attn_kernel(cvec_ref, qT_ref, k_ref, vT_ref, augk_ref, augq_ref, bias_ref, lq1_ref, lk1_ref, lq2_ref,
                 lk2_ref, gs_ref, o_ref, *scratch, tk, tq, nq, n_tiles):
    step = pl.program_id(0)
    per = TILES_PER_STEP
    s_len = k_ref.shape[0]
    nk = s_len // tk

    def tile_scalars(g):
        gh, t = _divmod_pow2(g, nq)
        j0 = t * tq
        return cvec_ref[_divmod_pow2(gh, N_HEADS)[1]], j0, _divmod_pow2(j0, tk)[0]

    def chunk_max(s):
        return jnp.max(s.reshape(tk // 8, 8, tq), axis=0)

    def pass1_steps(slot, g, i):
        c_h, j0, cd = tile_scalars(g)
        qT = qT_ref[:, i * tq:(i + 1) * tq]
        zq = jnp.zeros((HEAD_DIM, tq), BF16)
        qa = jnp.concatenate([qT[:HEAD_DIM], zq], axis=0)
        qb = jnp.concatenate([zq, qT[HEAD_DIM:]], axis=0)
        augq = augq_ref[0]
        w1 = jnp.concatenate([qa, augq], axis=0)
        w2 = jnp.concatenate([qb, augq], axis=0)
        m1 = jnp.full((8, tq), -1e30, F32)
        m2 = m1
        others = iter(range(nk - 1))
        for n in range(nk):
            if n == min(OWN_CHUNK_AT, nk - 1):
                k0 = pl.multiple_of(cd * tk, tk)
                bias = bias_ref[0, _divmod_pow2(_divmod_pow2(j0, tq)[0], tk // tq)[1]]
                kc = k_ref[pl.ds(k0, tk), :]
                s1 = _dot(kc, qa) + bias
                s2 = _dot(kc, qb) + bias
                off = 0.0
            else:
                ci = next(others)
                c = ci + (ci >= cd).astype(jnp.int32)
                after = (c > cd).astype(jnp.int32)
                k0 = pl.multiple_of(c * tk, tk)
                lhs = jnp.concatenate([k_ref[pl.ds(k0, tk), :], augk_ref[after]], axis=1)
                s1 = _dot(lhs, w1)
                s2 = _dot(lhs, w2)
                off = c_h * ((k0 - j0) * (1 - 2 * after)).astype(F32)
            slot.s[0, pl.ds(k0, tk), :] = s1
            slot.s[1, pl.ds(k0, tk), :] = s2
            m1 = jnp.maximum(m1, chunk_max(s1) + off)
            m2 = jnp.maximum(m2, chunk_max(s2) + off)
            if n == nk - 1:
                slot.m[0] = jnp.broadcast_to(jnp.max(m1, axis=0, keepdims=True), (8, tq))
                slot.m[1] = jnp.broadcast_to(jnp.max(m2, axis=0, keepdims=True), (8, tq))
            yield

    def pass2_steps(slot, g, i):
        c_h, j0, cd = tile_scalars(g)
        m1 = slot.m[0][0:1]
        m2 = slot.m[1][0:1]
        acc1 = jnp.zeros((V_ROWS, tq), F32)
        acc2 = acc1
        pending = []

        def value_matmul(acc1, acc2):
            cp, p1, p2 = pending.pop(0)
            vc = vT_ref[:, cp * tk:(cp + 1) * tk]
            return acc1 + _dot(vc, p1), acc2 + _dot(vc, p2)

        for c in range(nk):
            side = jnp.where(c < cd, 1, jnp.where(c > cd, -1, 0))
            off = c_h * (side * (c * tk - j0)).astype(F32)
            p1 = jnp.exp2(slot.s[0, c * tk:(c + 1) * tk, :] - (m1 - off)).astype(BF16)
            p2 = jnp.exp2(slot.s[1, c * tk:(c + 1) * tk, :] - (m2 - off)).astype(BF16)
            pending.append((c, p1, p2))
            if len(pending) > PV_LAG:
                acc1, acc2 = value_matmul(acc1, acc2)
            if c < nk - 1:
                yield
        while pending:
            acc1, acc2 = value_matmul(acc1, acc2)
        lam = (jnp.exp(jnp.sum(lq1_ref[...] * lk1_ref[...], axis=1, keepdims=True))
               - jnp.exp(jnp.sum(lq2_ref[...] * lk2_ref[...], axis=1, keepdims=True))
               + LAMBDA_INIT)
        o1 = acc1[:HEAD_W] / acc1[HEAD_W:HEAD_W + 1]
        o2 = acc2[:HEAD_W] / acc2[HEAD_W:HEAD_W + 1]
        o = o1 - lam * o2
        on = o * lax.rsqrt(jnp.mean(o * o, axis=0, keepdims=True) + RMS_EPS)
        o_ref[i * tq:(i + 1) * tq, :] = (on.T * (gs_ref[...] * (1.0 - LAMBDA_INIT))).astype(BF16)
        yield

    sets = [[_Slot(scratch[p * per + i], scratch[(2 + p) * per + i]) for i in range(per)]
            for p in range(2)]
    n_groups = n_tiles // per

    def pass1_group(p, u):
        return itertools.chain(*[pass1_steps(sets[p][i], u * per + i, i) for i in range(per)])

    def pass2_group(p, u):
        return itertools.chain(*[pass2_steps(sets[p][i], u * per + i, i) for i in range(per)])

    inner = jnp.logical_and(step > 0, step < n_groups)

    @pl.when(step == 0)
    def _():
        for _ in pass1_group(0, step):
            pass

    for parity in range(2):
        @pl.when(jnp.logical_and(inner, (step & 1) == parity))
        def _():
            older = pass2_group(1 - parity, step - 1)
            newer = pass1_group(parity, step)
            for _ in range(PASS2_LEAD):
                next(older)
            for _ in range(per * nk - PASS2_LEAD):
                next(older)
                next(newer)
            for _ in range(PASS2_LEAD):
                next(newer)

    @pl.when(step == n_groups)
    def _():
        for _ in pass2_group((n_groups - 1) % 2, step - 1):
            pass


def _alibi_tables(tk, tq):
    c = (2.0 ** (-8.0 * np.arange(1, N_HEADS + 1) / N_HEADS) * math.log2(math.e)).astype(np.float32)

    def split3(x):
        x1 = x.astype(BF16)
        r = x - x1.astype(np.float32)
        x2 = r.astype(BF16)
        x3 = (r - x2.astype(np.float32)).astype(BF16)
        return x1, x2, x3

    augk = np.zeros((N_HEADS, 2, tk, HEAD_W), BF16)
    augq = np.zeros((N_HEADS, HEAD_W, tq), BF16)
    u = c[:, None] * np.arange(tk, dtype=np.float32)[None, :]
    v = c[:, None] * np.arange(tq, dtype=np.float32)[None, :]
    for n, (un, vn) in enumerate(zip(split3(u), split3(v))):
        augk[:, 0, :, n] = un
        augk[:, 0, :, 3 + n] = 1.0
        augq[:, n, :] = 1.0
        augq[:, 3 + n, :] = -vn
    augk[:, 1] = -augk[:, 0]
    pos = np.arange(tk // tq, dtype=np.float32)[:, None, None] * tq
    dist = np.abs(np.arange(tk, dtype=np.float32)[None, :, None]
                  - np.arange(tq, dtype=np.float32)[None, None, :] - pos)
    bias = (-c[:, None, None, None] * dist[None]).astype(np.float32)
    return c, augk.reshape(2 * N_HEADS, tk, HEAD_W), augq, bias


def _tail_kernel(x_ref, yf_ref, ya_ref, p_ref, gmix_ref, wbg_ref, bbg_ref, wao_ref, wo_ref,
                 gffn_ref, wg_ref, wu_ref, wd_ref, gple_ref, wpg_ref, wpp_ref, gfin_ref, out_ref):
    d = x_ref.shape[1]
    x = x_ref[...]
    h = _rms(x, gmix_ref[...]).astype(BF16)
    gates = jax.nn.sigmoid(_dot(h, wbg_ref[...]) + bbg_ref[...])
    ya = _dot(ya_ref[...], wao_ref[...])
    merged = gates[:, :d] * yf_ref[...] + gates[:, d:] * ya
    x = x + _dot(merged.astype(BF16), wo_ref[...])

    h2 = _rms(x, gffn_ref[...]).astype(BF16)
    gg = _dot(h2, wg_ref[...])
    uu = _dot(h2, wu_ref[...])
    act = (gg * jax.nn.sigmoid(gg)) * uu
    x = x + _dot(act.astype(BF16), wd_ref[...])

    h3 = _rms(x, gple_ref[...]).astype(BF16)
    pg = jax.nn.sigmoid(_dot(h3, wpg_ref[...]))
    x = x + pg * _dot(p_ref[...].astype(BF16), wpp_ref[...])
    out_ref[...] = _rms(x, gfin_ref[...])


def _const_spec(shape):
    return pl.BlockSpec(shape, lambda *_: (0,) * len(shape), pipeline_mode=pl.Buffered(1))


def _dft_tables(s_len):
    n1 = math.isqrt(s_len)
    n2 = s_len // n1
    assert n1 * n2 == s_len and n1 % 16 == 0 and n2 % 16 == 0

    def cos_sin(m, size):
        ang = 2.0 * np.pi * (m % size) / size
        return np.cos(ang), np.sin(ang)

    c1, s1 = cos_sin(np.outer(np.arange(n1), np.arange(n1)), n1)
    f1 = (np.concatenate([c1, -s1], axis=0) * n1 ** -0.5).astype(np.float32)
    kk = np.arange(n1)[:, None, None] + n1 * np.arange(n2)[None, :, None]
    c2, s2 = cos_sin(kk * np.arange(n2)[None, None, :], s_len)
    f2 = (np.concatenate([np.concatenate([c2, s2], axis=2), np.concatenate([-s2, c2], axis=2)],
                         axis=1) * n2 ** -0.5).astype(np.float32)
    cc, sc = cos_sin(np.outer(np.arange(GROUP_DIM), np.arange(GROUP_DIM)), GROUP_DIM)
    cs = (np.concatenate([cc, sc], axis=0) * GROUP_DIM ** -0.5).astype(np.float32)
    return n1, n2, f1, f2, cs


def kernel(x, p, g_mix, w_in, lambda_q1, lambda_k1, lambda_q2, lambda_k2, g_subln,
           w_fourier_out, w_attn_out, w_branch_gate, b_branch_gate, w_o, g_ffn, w_ffn_gate,
           w_ffn_up, w_ffn_down, g_ple, w_ple_gate, w_ple_proj, g_final):
    bsz, s_len, d = x.shape
    t_tok = bsz * s_len
    d_ff = w_ffn_gate.shape[-1]
    ple = p.shape[-1]
    assert w_in.shape[0] == 1 and d == N_HEADS * HEAD_W
    x2 = x.reshape(t_tok, d)
    p2 = p[0].reshape(t_tok, ple)
    n1, n2, f1, f2, cs = _dft_tables(s_len)
    cparams = functools.partial(pltpu.CompilerParams, vmem_limit_bytes=VMEM_LIMIT)

    tm = min(512, s_len)
    row = lambda i: (i, 0)
    col = lambda i: (0, i)
    u_f, qT, k_arr, vT = pl.pallas_call(
        _in_proj_kernel,
        grid=(t_tok // tm,),
        in_specs=[pl.BlockSpec((tm, d), row), _const_spec((1, d)), _const_spec((d, 4 * d))],
        out_specs=[pl.BlockSpec((tm, d), row), pl.BlockSpec((d, tm), col),
                   pl.BlockSpec((tm, d), row), pl.BlockSpec((N_HEADS * V_ROWS, tm), col)],
        out_shape=[jax.ShapeDtypeStruct((t_tok, d), F32), jax.ShapeDtypeStruct((d, t_tok), BF16),
                   jax.ShapeDtypeStruct((t_tok, d), BF16),
                   jax.ShapeDtypeStruct((N_HEADS * V_ROWS, t_tok), BF16)],
        compiler_params=cparams(dimension_semantics=("arbitrary",)),
        name="in_proj",
    )(x2, g_mix[0][None, :], w_in[0].astype(BF16))

    r = min(FFT1_ROWS, n2)
    blk = pl.BlockSpec((n1, r, d), lambda b, j: (b, j, 0))
    t_re, t_im = pl.pallas_call(
        _fft1_kernel,
        grid=(bsz, n2 // r),
        in_specs=[blk, _const_spec((2 * n1, n1))],
        out_specs=[blk, blk],
        out_shape=[jax.ShapeDtypeStruct((bsz * n1, n2, d), BF16)] * 2,
        compiler_params=cparams(dimension_semantics=("arbitrary", "arbitrary")),
        name="fft_stage1",
    )(u_f.reshape(bsz * n1, n2, d), jnp.asarray(f1).astype(BF16))

    grp = min(FFT2_GROUP, n1)
    ngrp = n1 // grp
    tblk = pl.BlockSpec((grp * n2, d), row)
    y_f = pl.pallas_call(
        functools.partial(_fft2_kernel, n2=n2),
        grid=(bsz * ngrp,),
        in_specs=[tblk, tblk, pl.BlockSpec((grp, 2 * n2, 2 * n2), lambda s: (s % ngrp, 0, 0)),
                  _const_spec((2 * GROUP_DIM, GROUP_DIM)), _const_spec((d, d))],
        out_specs=pl.BlockSpec((n2, grp, d), lambda s: (s // ngrp, s % ngrp, 0)),
        out_shape=jax.ShapeDtypeStruct((bsz * n2, n1, d), F32),
        compiler_params=cparams(dimension_semantics=("arbitrary",)),
        name="fft_stage2",
    )(t_re.reshape(t_tok, d), t_im.reshape(t_tok, d), jnp.asarray(f2).astype(BF16),
      jnp.asarray(cs).astype(BF16),
      w_fourier_out[0].astype(BF16))
    y_f = y_f.reshape(t_tok, d)

    tq = min(256, s_len)
    tk = min(512, s_len)
    nq = s_len // tq
    cvec, augk, augq, bias = _alibi_tables(tk, tq)
    n_tiles = bsz * N_HEADS * nq

    per = TILES_PER_STEP
    npq = nq // per
    n_groups = n_tiles // per
    assert nq % per == 0

    def group(u):
        uh, tg = _divmod_pow2(u, npq)
        return _divmod_pow2(uh, N_HEADS) + (tg,)

    cur = lambda u: group(jnp.minimum(u, n_groups - 1))
    prev = lambda u: group(jnp.maximum(u - 1, 0))
    lam_spec = pl.BlockSpec((1, HEAD_DIM), lambda u: (0, 0))
    y_a = pl.pallas_call(
        functools.partial(_attn_kernel, tk=tk, tq=tq, nq=nq, n_tiles=n_tiles),
        grid=(n_groups + 1,),
        in_specs=[pl.BlockSpec(memory_space=pltpu.SMEM),
                  pl.BlockSpec((HEAD_W, per * tq), lambda u: (cur(u)[1], cur(u)[0] * npq + cur(u)[2])),
                  pl.BlockSpec((s_len, HEAD_W), lambda u: (cur(u)[0], cur(u)[1])),
                  pl.BlockSpec((V_ROWS, s_len), lambda u: (prev(u)[1], prev(u)[0])),
                  pl.BlockSpec((2, tk, HEAD_W), lambda u: (cur(u)[1], 0, 0)),
                  pl.BlockSpec((1, HEAD_W, tq), lambda u: (cur(u)[1], 0, 0)),
                  pl.BlockSpec((1, tk // tq, tk, tq), lambda u: (cur(u)[1], 0, 0, 0)),
                  lam_spec, lam_spec, lam_spec, lam_spec,
                  pl.BlockSpec((1, HEAD_W), lambda u: (0, 0))],
        out_specs=pl.BlockSpec((per * tq, HEAD_W),
                               lambda u: (prev(u)[0] * npq + prev(u)[2], prev(u)[1])),
        out_shape=jax.ShapeDtypeStruct((t_tok, d), BF16),
        scratch_shapes=([pltpu.VMEM((2, s_len, tq), F32)] * (2 * per)
                        + [pltpu.VMEM((2, 8, tq), F32)] * (2 * per)),
        compiler_params=cparams(dimension_semantics=("arbitrary",)),
        name="diff_attn",
    )(jnp.asarray(cvec), qT, k_arr, vT, jnp.asarray(augk), jnp.asarray(augq), jnp.asarray(bias),
      lambda_q1, lambda_k1, lambda_q2, lambda_k2, g_subln)

    tt = min(512, s_len)
    vec = lambda a: a.reshape(1, -1)
    out = pl.pallas_call(
        _tail_kernel,
        grid=(t_tok // tt,),
        in_specs=[pl.BlockSpec((tt, d), row), pl.BlockSpec((tt, d), row),
                  pl.BlockSpec((tt, d), row), pl.BlockSpec((tt, ple), row),
                  _const_spec((1, d)), _const_spec((d, 2 * d)), _const_spec((1, 2 * d)),
                  _const_spec((d, d)), _const_spec((d, d)),
                  _const_spec((1, d)), _const_spec((d, d_ff)), _const_spec((d, d_ff)),
                  _const_spec((d_ff, d)),
                  _const_spec((1, d)), _const_spec((d, d)), _const_spec((ple, d)),
                  _const_spec((1, d))],
        out_specs=pl.BlockSpec((tt, d), row),
        out_shape=jax.ShapeDtypeStruct((t_tok, d), F32),
        compiler_params=cparams(dimension_semantics=("arbitrary",)),
        name="tail",
    )(x2, y_f, y_a, p2, vec(g_mix[0]), w_branch_gate[0].astype(BF16), vec(b_branch_gate[0]),
      w_attn_out[0].astype(BF16), w_o[0].astype(BF16), vec(g_ffn[0]),
      w_ffn_gate[0].astype(BF16), w_ffn_up[0].astype(BF16), w_ffn_down[0].astype(BF16),
      vec(g_ple[0]), w_ple_gate[0].astype(BF16), w_ple_proj[0].astype(BF16), vec(g_final))
    return out.reshape(bsz, s_len, d)
```

```python
import functools
import itertools
import math
from typing import Any, NamedTuple

import jax
import jax.numpy as jnp
import numpy as np
from jax import lax
from jax.experimental import pallas as pl
from jax.experimental.pallas import tpu as pltpu

F32 = jnp.float32
BF16 = jnp.bfloat16

N_HEADS = 8
HEAD_DIM = 64
HEAD_W = 2 * HEAD_DIM
GROUP_DIM = 128
RMS_EPS = 1e-6
LAMBDA_INIT = 0.8 - 0.6 * math.exp(-0.3 * 0)
V_ROWS = HEAD_W + 16
VMEM_LIMIT = 56 * 1024 * 1024
FFT1_ROWS = 16
FFT2_GROUP = 8
TILES_PER_STEP = 2
PV_LAG = 2
PASS2_LEAD = 1
OWN_CHUNK_AT = 8
Q_SCALE = HEAD_DIM ** -0.5 * math.log2(math.e)


def _rms(x, g):
    return x * lax.rsqrt(jnp.mean(x * x, axis=-1, keepdims=True) + RMS_EPS) * g


def _dot(a, b):
    return jnp.dot(a, b, preferred_element_type=F32)


def _divmod_pow2(x, n):
    assert n > 0 and n & (n - 1) == 0, n
    return x >> (n.bit_length() - 1), x & (n - 1)


def _in_proj_kernel(x_ref, g_ref, w_ref, uf_ref, qT_ref, k_ref, vT_ref):
    d = x_ref.shape[1]
    tm = x_ref.shape[0]
    h = _rms(x_ref[...], g_ref[...]).astype(BF16)
    uf_ref[...] = _dot(h, w_ref[:, 0:d])
    q = _dot(h, w_ref[:, d:2 * d]) * Q_SCALE
    qT_ref[...] = q.T.astype(BF16)
    k_ref[...] = _dot(h, w_ref[:, 2 * d:3 * d]).astype(BF16)
    vT = _dot(h, w_ref[:, 3 * d:4 * d]).T
    ones = jnp.ones((V_ROWS - HEAD_W, tm), BF16)
    for hh in range(N_HEADS):
        vT_ref[hh * V_ROWS:hh * V_ROWS + HEAD_W, :] = vT[hh * HEAD_W:(hh + 1) * HEAD_W, :].astype(BF16)
        vT_ref[hh * V_ROWS + HEAD_W:(hh + 1) * V_ROWS, :] = ones


def _fft1_kernel(x_ref, f1_ref, tre_ref, tim_ref):
    n1, r, d = x_ref.shape
    xs = jnp.swapaxes(x_ref[...], 0, 1)
    f1 = f1_ref[...]
    ts = [_dot(f1, xs[q].astype(BF16)) for q in range(r)]
    tre_ref[...] = jnp.swapaxes(jnp.stack([t[:n1] for t in ts], axis=0), 0, 1).astype(BF16)
    tim_ref[...] = jnp.swapaxes(jnp.stack([t[n1:] for t in ts], axis=0), 0, 1).astype(BF16)


def _fft2_kernel(tre_ref, tim_ref, f2_ref, cs_ref, wfo_ref, y_ref, *, n2):
    d = tre_ref.shape[1]
    grp = tre_ref.shape[0] // n2
    zs = []
    for i in range(grp):
        rhs = jnp.concatenate([tre_ref[i * n2:(i + 1) * n2, :], tim_ref[i * n2:(i + 1) * n2, :]], axis=0)
        zs.append(_dot(f2_ref[i], rhs))
    re = jnp.concatenate([z[:n2] for z in zs], axis=0).astype(BF16)
    im = jnp.concatenate([z[n2:] for z in zs], axis=0).astype(BF16)
    cs = cs_ref[...]
    ys = []
    for g in range(d // GROUP_DIM):
        lo, hi = g * GROUP_DIM, (g + 1) * GROUP_DIM
        ys.append(_dot(jnp.concatenate([re[:, lo:hi], im[:, lo:hi]], axis=1), cs))
    y = jnp.concatenate(ys, axis=1).astype(BF16)
    out = _dot(y, wfo_ref[...])
    y_ref[...] = jnp.swapaxes(out.reshape(grp, n2, d), 0, 1)


class _Slot(NamedTuple):
    s: Any
    m: Any


def _attn_kernel(cvec_ref, qT_ref, k_ref, vT_ref, augk_ref, augq_ref, bias_ref, lq1_ref, lk1_ref, lq2_ref,
                 lk2_ref, gs_ref, o_ref, *scratch, tk, tq, nq, n_tiles):
    step = pl.program_id(0)
    per = TILES_PER_STEP
    s_len = k_ref.shape[0]
    nk = s_len // tk

    def tile_scalars(g):
        gh, t = _divmod_pow2(g, nq)
        j0 = t * tq
        return cvec_ref[_divmod_pow2(gh, N_HEADS)[1]], j0, _divmod_pow2(j0, tk)[0]

    def chunk_max(s):
        return jnp.max(s.reshape(tk // 8, 8, tq), axis=0)

    def pass1_steps(slot, g, i):
        c_h, j0, cd = tile_scalars(g)
        qT = qT_ref[:, i * tq:(i + 1) * tq]
        zq = jnp.zeros((HEAD_DIM, tq), BF16)
        qa = jnp.concatenate([qT[:HEAD_DIM], zq], axis=0)
        qb = jnp.concatenate([zq, qT[HEAD_DIM:]], axis=0)
        augq = augq_ref[0]
        w1 = jnp.concatenate([qa, augq], axis=0)
        w2 = jnp.concatenate([qb, augq], axis=0)
        m1 = jnp.full((8, tq), -1e30, F32)
        m2 = m1
        others = iter(range(nk - 1))
        for n in range(nk):
            if n == min(OWN_CHUNK_AT, nk - 1):
                k0 = pl.multiple_of(cd * tk, tk)
                bias = bias_ref[0, _divmod_pow2(_divmod_pow2(j0, tq)[0], tk // tq)[1]]
                kc = k_ref[pl.ds(k0, tk), :]
                s1 = _dot(kc, qa) + bias
                s2 = _dot(kc, qb) + bias
                off = 0.0
            else:
                ci = next(others)
                c = ci + (ci >= cd).astype(jnp.int32)
                after = (c > cd).astype(jnp.int32)
                k0 = pl.multiple_of(c * tk, tk)
                lhs = jnp.concatenate([k_ref[pl.ds(k0, tk), :], augk_ref[after]], axis=1)
                s1 = _dot(lhs, w1)
                s2 = _dot(lhs, w2)
                off = c_h * ((k0 - j0) * (1 - 2 * after)).astype(F32)
            slot.s[0, pl.ds(k0, tk), :] = s1
            slot.s[1, pl.ds(k0, tk), :] = s2
            m1 = jnp.maximum(m1, chunk_max(s1) + off)
            m2 = jnp.maximum(m2, chunk_max(s2) + off)
            if n == nk - 1:
                slot.m[0] = jnp.broadcast_to(jnp.max(m1, axis=0, keepdims=True), (8, tq))
                slot.m[1] = jnp.broadcast_to(jnp.max(m2, axis=0, keepdims=True), (8, tq))
            yield

    def pass2_steps(slot, g, i):
        c_h, j0, cd = tile_scalars(g)
        m1 = slot.m[0][0:1]
        m2 = slot.m[1][0:1]
        acc1 = jnp.zeros((V_ROWS, tq), F32)
        acc2 = acc1
        pending = []

        def value_matmul(acc1, acc2):
            cp, p1, p2 = pending.pop(0)
            vc = vT_ref[:, cp * tk:(cp + 1) * tk]
            return acc1 + _dot(vc, p1), acc2 + _dot(vc, p2)

        for c in range(nk):
            side = jnp.where(c < cd, 1, jnp.where(c > cd, -1, 0))
            off = c_h * (side * (c * tk - j0)).astype(F32)
            p1 = jnp.exp2(slot.s[0, c * tk:(c + 1) * tk, :] - (m1 - off)).astype(BF16)
            p2 = jnp.exp2(slot.s[1, c * tk:(c + 1) * tk, :] - (m2 - off)).astype(BF16)
            pending.append((c, p1, p2))
            if len(pending) > PV_LAG:
                acc1, acc2 = value_matmul(acc1, acc2)
            if c < nk - 1:
                yield
        while pending:
            acc1, acc2 = value_matmul(acc1, acc2)
        lam = (jnp.exp(jnp.sum(lq1_ref[...] * lk1_ref[...], axis=1, keepdims=True))
               - jnp.exp(jnp.sum(lq2_ref[...] * lk2_ref[...], axis=1, keepdims=True))
               + LAMBDA_INIT)
        o1 = acc1[:HEAD_W] / acc1[HEAD_W:HEAD_W + 1]
        o2 = acc2[:HEAD_W] / acc2[HEAD_W:HEAD_W + 1]
        o = o1 - lam * o2
        on = o * lax.rsqrt(jnp.mean(o * o, axis=0, keepdims=True) + RMS_EPS)
        o_ref[i * tq:(i + 1) * tq, :] = (on.T * (gs_ref[...] * (1.0 - LAMBDA_INIT))).astype(BF16)
        yield

    sets = [[_Slot(scratch[p * per + i], scratch[(2 + p) * per + i]) for i in range(per)]
            for p in range(2)]
    n_groups = n_tiles // per

    def pass1_group(p, u):
        return itertools.chain(*[pass1_steps(sets[p][i], u * per + i, i) for i in range(per)])

    def pass2_group(p, u):
        return itertools.chain(*[pass2_steps(sets[p][i], u * per + i, i) for i in range(per)])

    inner = jnp.logical_and(step > 0, step < n_groups)

    @pl.when(step == 0)
    def _():
        for _ in pass1_group(0, step):
            pass

    for parity in range(2):
        @pl.when(jnp.logical_and(inner, (step & 1) == parity))
        def _():
            older = pass2_group(1 - parity, step - 1)
            newer = pass1_group(parity, step)
            for _ in range(PASS2_LEAD):
                next(older)
            for _ in range(per * nk - PASS2_LEAD):
                next(older)
                next(newer)
            for _ in range(PASS2_LEAD):
                next(newer)

    @pl.when(step == n_groups)
    def _():
        for _ in pass2_group((n_groups - 1) % 2, step - 1):
            pass


def _alibi_tables(tk, tq):
    c = (2.0 ** (-8.0 * np.arange(1, N_HEADS + 1) / N_HEADS) * math.log2(math.e)).astype(np.float32)

    def split3(x):
        x1 = x.astype(BF16)
        r = x - x1.astype(np.float32)
        x2 = r.astype(BF16)
        x3 = (r - x2.astype(np.float32)).astype(BF16)
        return x1, x2, x3

    augk = np.zeros((N_HEADS, 2, tk, HEAD_W), BF16)
    augq = np.zeros((N_HEADS, HEAD_W, tq), BF16)
    u = c[:, None] * np.arange(tk, dtype=np.float32)[None, :]
    v = c[:, None] * np.arange(tq, dtype=np.float32)[None, :]
    for n, (un, vn) in enumerate(zip(split3(u), split3(v))):
        augk[:, 0, :, n] = un
        augk[:, 0, :, 3 + n] = 1.0
        augq[:, n, :] = 1.0
        augq[:, 3 + n, :] = -vn
    augk[:, 1] = -augk[:, 0]
    pos = np.arange(tk // tq, dtype=np.float32)[:, None, None] * tq
    dist = np.abs(np.arange(tk, dtype=np.float32)[None, :, None]
                  - np.arange(tq, dtype=np.float32)[None, None, :] - pos)
    bias = (-c[:, None, None, None] * dist[None]).astype(np.float32)
    return c, augk.reshape(2 * N_HEADS, tk, HEAD_W), augq, bias


def _tail_kernel(x_ref, yf_ref, ya_ref, p_ref, gmix_ref, wbg_ref, bbg_ref, wao_ref, wo_ref,
                 gffn_ref, wg_ref, wu_ref, wd_ref, gple_ref, wpg_ref, wpp_ref, gfin_ref, out_ref):
    d = x_ref.shape[1]
    x = x_ref[...]
    h = _rms(x, gmix_ref[...]).astype(BF16)
    gates = jax.nn.sigmoid(_dot(h, wbg_ref[...]) + bbg_ref[...])
    ya = _dot(ya_ref[...], wao_ref[...])
    merged = gates[:, :d] * yf_ref[...] + gates[:, d:] * ya
    x = x + _dot(merged.astype(BF16), wo_ref[...])

    h2 = _rms(x, gffn_ref[...]).astype(BF16)
    gg = _dot(h2, wg_ref[...])
    uu = _dot(h2, wu_ref[...])
    act = (gg * jax.nn.sigmoid(gg)) * uu
    x = x + _dot(act.astype(BF16), wd_ref[...])

    h3 = _rms(x, gple_ref[...]).astype(BF16)
    pg = jax.nn.sigmoid(_dot(h3, wpg_ref[...]))
    x = x + pg * _dot(p_ref[...].astype(BF16), wpp_ref[...])
    out_ref[...] = _rms(x, gfin_ref[...])


def _const_spec(shape):
    return pl.BlockSpec(shape, lambda *_: (0,) * len(shape), pipeline_mode=pl.Buffered(1))


def _dft_tables(s_len):
    n1 = math.isqrt(s_len)
    n2 = s_len // n1
    assert n1 * n2 == s_len and n1 % 16 == 0 and n2 % 16 == 0

    def cos_sin(m, size):
        ang = 2.0 * np.pi * (m % size) / size
        return np.cos(ang), np.sin(ang)

    c1, s1 = cos_sin(np.outer(np.arange(n1), np.arange(n1)), n1)
    f1 = (np.concatenate([c1, -s1], axis=0) * n1 ** -0.5).astype(np.float32)
    kk = np.arange(n1)[:, None, None] + n1 * np.arange(n2)[None, :, None]
    c2, s2 = cos_sin(kk * np.arange(n2)[None, None, :], s_len)
    f2 = (np.concatenate([np.concatenate([c2, s2], axis=2), np.concatenate([-s2, c2], axis=2)],
                         axis=1) * n2 ** -0.5).astype(np.float32)
    cc, sc = cos_sin(np.outer(np.arange(GROUP_DIM), np.arange(GROUP_DIM)), GROUP_DIM)
    cs = (np.concatenate([cc, sc], axis=0) * GROUP_DIM ** -0.5).astype(np.float32)
    return n1, n2, f1, f2, cs


def kernel(x, p, g_mix, w_in, lambda_q1, lambda_k1, lambda_q2, lambda_k2, g_subln,
           w_fourier_out, w_attn_out, w_branch_gate, b_branch_gate, w_o, g_ffn, w_ffn_gate,
           w_ffn_up, w_ffn_down, g_ple, w_ple_gate, w_ple_proj, g_final):
    bsz, s_len, d = x.shape
    t_tok = bsz * s_len
    d_ff = w_ffn_gate.shape[-1]
    ple = p.shape[-1]
    assert w_in.shape[0] == 1 and d == N_HEADS * HEAD_W
    x2 = x.reshape(t_tok, d)
    p2 = p[0].reshape(t_tok, ple)
    n1, n2, f1, f2, cs = _dft_tables(s_len)
    cparams = functools.partial(pltpu.CompilerParams, vmem_limit_bytes=VMEM_LIMIT)

    tm = min(512, s_len)
    row = lambda i: (i, 0)
    col = lambda i: (0, i)
    u_f, qT, k_arr, vT = pl.pallas_call(
        _in_proj_kernel,
        grid=(t_tok // tm,),
        in_specs=[pl.BlockSpec((tm, d), row), _const_spec((1, d)), _const_spec((d, 4 * d))],
        out_specs=[pl.BlockSpec((tm, d), row), pl.BlockSpec((d, tm), col),
                   pl.BlockSpec((tm, d), row), pl.BlockSpec((N_HEADS * V_ROWS, tm), col)],
        out_shape=[jax.ShapeDtypeStruct((t_tok, d), F32), jax.ShapeDtypeStruct((d, t_tok), BF16),
                   jax.ShapeDtypeStruct((t_tok, d), BF16),
                   jax.ShapeDtypeStruct((N_HEADS * V_ROWS, t_tok), BF16)],
        compiler_params=cparams(dimension_semantics=("arbitrary",)),
        name="in_proj",
    )(x2, g_mix[0][None, :], w_in[0].astype(BF16))

    r = min(FFT1_ROWS, n2)
    blk = pl.BlockSpec((n1, r, d), lambda b, j: (b, j, 0))
    t_re, t_im = pl.pallas_call(
        _fft1_kernel,
        grid=(bsz, n2 // r),
        in_specs=[blk, _const_spec((2 * n1, n1))],
        out_specs=[blk, blk],
        out_shape=[jax.ShapeDtypeStruct((bsz * n1, n2, d), BF16)] * 2,
        compiler_params=cparams(dimension_semantics=("arbitrary", "arbitrary")),
        name="fft_stage1",
    )(u_f.reshape(bsz * n1, n2, d), jnp.asarray(f1).astype(BF16))

    grp = min(FFT2_GROUP, n1)
    ngrp = n1 // grp
    tblk = pl.BlockSpec((grp * n2, d), row)
    y_f = pl.pallas_call(
        functools.partial(_fft2_kernel, n2=n2),
        grid=(bsz * ngrp,),
        in_specs=[tblk, tblk, pl.BlockSpec((grp, 2 * n2, 2 * n2), lambda s: (s % ngrp, 0, 0)),
                  _const_spec((2 * GROUP_DIM, GROUP_DIM)), _const_spec((d, d))],
        out_specs=pl.BlockSpec((n2, grp, d), lambda s: (s // ngrp, s % ngrp, 0)),
        out_shape=jax.ShapeDtypeStruct((bsz * n2, n1, d), F32),
        compiler_params=cparams(dimension_semantics=("arbitrary",)),
        name="fft_stage2",
    )(t_re.reshape(t_tok, d), t_im.reshape(t_tok, d), jnp.asarray(f2).astype(BF16),
      jnp.asarray(cs).astype(BF16),
      w_fourier_out[0].astype(BF16))
    y_f = y_f.reshape(t_tok, d)

    tq = min(256, s_len)
    tk = min(256, s_len)
    nq = s_len // tq
    cvec, augk, augq, bias = _alibi_tables(tk, tq)
    n_tiles = bsz * N_HEADS * nq

    per = TILES_PER_STEP
    npq = nq // per
    n_groups = n_tiles // per
    assert nq % per == 0

    def group(u):
        uh, tg = _divmod_pow2(u, npq)
        return _divmod_pow2(uh, N_HEADS) + (tg,)

    cur = lambda u: group(jnp.minimum(u, n_groups - 1))
    prev = lambda u: group(jnp.maximum(u - 1, 0))
    lam_spec = pl.BlockSpec((1, HEAD_DIM), lambda u: (0, 0))
    y_a = pl.pallas_call(
        functools.partial(_attn_kernel, tk=tk, tq=tq, nq=nq, n_tiles=n_tiles),
        grid=(n_groups + 1,),
        in_specs=[pl.BlockSpec(memory_space=pltpu.SMEM),
                  pl.BlockSpec((HEAD_W, per * tq), lambda u: (cur(u)[1], cur(u)[0] * npq + cur(u)[2])),
                  pl.BlockSpec((s_len, HEAD_W), lambda u: (cur(u)[0], cur(u)[1])),
                  pl.BlockSpec((V_ROWS, s_len), lambda u: (prev(u)[1], prev(u)[0])),
                  pl.BlockSpec((2, tk, HEAD_W), lambda u: (cur(u)[1], 0, 0)),
                  pl.BlockSpec((1, HEAD_W, tq), lambda u: (cur(u)[1], 0, 0)),
                  pl.BlockSpec((1, tk // tq, tk, tq), lambda u: (cur(u)[1], 0, 0, 0)),
                  lam_spec, lam_spec, lam_spec, lam_spec,
                  pl.BlockSpec((1, HEAD_W), lambda u: (0, 0))],
        out_specs=pl.BlockSpec((per * tq, HEAD_W),
                               lambda u: (prev(u)[0] * npq + prev(u)[2], prev(u)[1])),
        out_shape=jax.ShapeDtypeStruct((t_tok, d), BF16),
        scratch_shapes=([pltpu.VMEM((2, s_len, tq), F32)] * (2 * per)
                        + [pltpu.VMEM((2, 8, tq), F32)] * (2 * per)),
        compiler_params=cparams(dimension_semantics=("arbitrary",)),
        name="diff_attn",
    )(jnp.asarray(cvec), qT, k_arr, vT, jnp.asarray(augk), jnp.asarray(augq), jnp.asarray(bias),
      lambda_q1, lambda_k1, lambda_q2, lambda_k2, g_subln)

    tt = min(512, s_len)
    vec = lambda a: a.reshape(1, -1)
    out = pl.pallas_call(
        _tail_kernel,
        grid=(t_tok // tt,),
        in_specs=[pl.BlockSpec((tt, d), row), pl.BlockSpec((tt, d), row),
                  pl.BlockSpec((tt, d), row), pl.BlockSpec((tt, ple), row),
                  _const_spec((1, d)), _const_spec((d, 2 * d)), _const_spec((1, 2 * d)),
                  _const_spec((d, d)), _const_spec((d, d)),
                  _const_spec((1, d)), _const_spec((d, d_ff)), _const_spec((d, d_ff)),
                  _const_spec((d_ff, d)),
                  _const_spec((1, d)), _const_spec((d, d)), _const_spec((ple, d)),
                  _const_spec((1, d))],
        out_specs=pl.BlockSpec((tt, d), row),
        out_shape=jax.ShapeDtypeStruct((t_tok, d), F32),
        compiler_params=cparams(dimension_semantics=("arbitrary",)),
        name="tail",
    )(x2, y_f, y_a, p2, vec(g_mix[0]), w_branch_gate[0].astype(BF16), vec(b_branch_gate[0]),
      w_attn_out[0].astype(BF16), w_o[0].astype(BF16), vec(g_ffn[0]),
      w_ffn_gate[0].astype(BF16), w_ffn_up[0].astype(BF16), w_ffn_down[0].astype(BF16),
      vec(g_ple[0]), w_ple_gate[0].astype(BF16), w_ple_proj[0].astype(BF16), vec(g_final))
    return out.reshape(bsz, s_len, d)
```

```python
import functools
import itertools
import math
from typing import Any, NamedTuple

import jax
import jax.numpy as jnp
import numpy as np
from jax import lax
from jax.experimental import pallas as pl
from jax.experimental.pallas import tpu as pltpu

F32 = jnp.float32
BF16 = jnp.bfloat16

N_HEADS = 8
HEAD_DIM = 64
HEAD_W = 2 * HEAD_DIM
GROUP_DIM = 128
RMS_EPS = 1e-6
LAMBDA_INIT = 0.8 - 0.6 * math.exp(-0.3 * 0)
V_ROWS = HEAD_W + 16
VMEM_LIMIT = 56 * 1024 * 1024
FFT1_ROWS = 16
FFT2_GROUP = 16
TILES_PER_STEP = 2
PV_LAG = 2
PASS2_LEAD = 1
OWN_CHUNK_AT = 8
Q_SCALE = HEAD_DIM ** -0.5 * math.log2(math.e)


def _rms(x, g):
    return x * lax.rsqrt(jnp.mean(x * x, axis=-1, keepdims=True) + RMS_EPS) * g


def _dot(a, b):
    return jnp.dot(a, b, preferred_element_type=F32)


def _divmod_pow2(x, n):
    assert n > 0 and n & (n - 1) == 0, n
    return x >> (n.bit_length() - 1), x & (n - 1)


def _in_proj_kernel(x_ref, g_ref, w_ref, uf_ref, qT_ref, k_ref, vT_ref):
    d = x_ref.shape[1]
    tm = x_ref.shape[0]
    h = _rms(x_ref[...], g_ref[...]).astype(BF16)
    uf_ref[...] = _dot(h, w_ref[:, 0:d])
    q = _dot(h, w_ref[:, d:2 * d]) * Q_SCALE
    qT_ref[...] = q.T.astype(BF16)
    k_ref[...] = _dot(h, w_ref[:, 2 * d:3 * d]).astype(BF16)
    vT = _dot(h, w_ref[:, 3 * d:4 * d]).T
    ones = jnp.ones((V_ROWS - HEAD_W, tm), BF16)
    for hh in range(N_HEADS):
        vT_ref[hh * V_ROWS:hh * V_ROWS + HEAD_W, :] = vT[hh * HEAD_W:(hh + 1) * HEAD_W, :].astype(BF16)
        vT_ref[hh * V_ROWS + HEAD_W:(hh + 1) * V_ROWS, :] = ones


def _fft1_kernel(x_ref, f1_ref, tre_ref, tim_ref):
    n1, r, d = x_ref.shape
    xs = jnp.swapaxes(x_ref[...], 0, 1)
    f1 = f1_ref[...]
    ts = [_dot(f1, xs[q].astype(BF16)) for q in range(r)]
    tre_ref[...] = jnp.swapaxes(jnp.stack([t[:n1] for t in ts], axis=0), 0, 1).astype(BF16)
    tim_ref[...] = jnp.swapaxes(jnp.stack([t[n1:] for t in ts], axis=0), 0, 1).astype(BF16)


def _fft2_kernel(tre_ref, tim_ref, f2_ref, cs_ref, wfo_ref, y_ref, *, n2):
    d = tre_ref.shape[1]
    grp = tre_ref.shape[0] // n2
    zs = []
    for i in range(grp):
        rhs = jnp.concatenate([tre_ref[i * n2:(i + 1) * n2, :], tim_ref[i * n2:(i + 1) * n2, :]], axis=0)
        zs.append(_dot(f2_ref[i], rhs))
    re = jnp.concatenate([z[:n2] for z in zs], axis=0).astype(BF16)
    im = jnp.concatenate([z[n2:] for z in zs], axis=0).astype(BF16)
    cs = cs_ref[...]
    ys = []
    for g in range(d // GROUP_DIM):
        lo, hi = g * GROUP_DIM, (g + 1) * GROUP_DIM
        ys.append(_dot(jnp.concatenate([re[:, lo:hi], im[:, lo:hi]], axis=1), cs))
    y = jnp.concatenate(ys, axis=1).astype(BF16)
    out = _dot(y, wfo_ref[...])
    y_ref[...] = jnp.swapaxes(out.reshape(grp, n2, d), 0, 1)


class _Slot(NamedTuple):
    s: Any
    m: Any


def _attn_kernel(cvec_ref, qT_ref, k_ref, vT_ref, augk_ref, augq_ref, bias_ref, lq1_ref, lk1_ref, lq2_ref,
                 lk2_ref, gs_ref, o_ref, *scratch, tk, tq, nq, n_tiles):
    step = pl.program_id(0)
    per = TILES_PER_STEP
    s_len = k_ref.shape[0]
    nk = s_len // tk

    def tile_scalars(g):
        gh, t = _divmod_pow2(g, nq)
        j0 = t * tq
        return cvec_ref[_divmod_pow2(gh, N_HEADS)[1]], j0, _divmod_pow2(j0, tk)[0]

    def chunk_max(s):
        return jnp.max(s.reshape(tk // 8, 8, tq), axis=0)

    def pass1_steps(slot, g, i):
        c_h, j0, cd = tile_scalars(g)
        qT = qT_ref[:, i * tq:(i + 1) * tq]
        zq = jnp.zeros((HEAD_DIM, tq), BF16)
        qa = jnp.concatenate([qT[:HEAD_DIM], zq], axis=0)
        qb = jnp.concatenate([zq, qT[HEAD_DIM:]], axis=0)
        augq = augq_ref[0]
        w1 = jnp.concatenate([qa, augq], axis=0)
        w2 = jnp.concatenate([qb, augq], axis=0)
        m1 = jnp.full((8, tq), -1e30, F32)
        m2 = m1
        others = iter(range(nk - 1))
        for n in range(nk):
            if n == min(OWN_CHUNK_AT, nk - 1):
                k0 = pl.multiple_of(cd * tk, tk)
                bias = bias_ref[0, _divmod_pow2(_divmod_pow2(j0, tq)[0], tk // tq)[1]]
                kc = k_ref[pl.ds(k0, tk), :]
                s1 = _dot(kc, qa) + bias
                s2 = _dot(kc, qb) + bias
                off = 0.0
            else:
                ci = next(others)
                c = ci + (ci >= cd).astype(jnp.int32)
                after = (c > cd).astype(jnp.int32)
                k0 = pl.multiple_of(c * tk, tk)
                lhs = jnp.concatenate([k_ref[pl.ds(k0, tk), :], augk_ref[after]], axis=1)
                s1 = _dot(lhs, w1)
                s2 = _dot(lhs, w2)
                off = c_h * ((k0 - j0) * (1 - 2 * after)).astype(F32)
            slot.s[0, pl.ds(k0, tk), :] = s1
            slot.s[1, pl.ds(k0, tk), :] = s2
            m1 = jnp.maximum(m1, chunk_max(s1) + off)
            m2 = jnp.maximum(m2, chunk_max(s2) + off)
            if n == nk - 1:
                slot.m[0] = jnp.broadcast_to(jnp.max(m1, axis=0, keepdims=True), (8, tq))
                slot.m[1] = jnp.broadcast_to(jnp.max(m2, axis=0, keepdims=True), (8, tq))
            yield

    def pass2_steps(slot, g, i):
        c_h, j0, cd = tile_scalars(g)
        m1 = slot.m[0][0:1]
        m2 = slot.m[1][0:1]
        acc1 = jnp.zeros((V_ROWS, tq), F32)
        acc2 = acc1
        pending = []

        def value_matmul(acc1, acc2):
            cp, p1, p2 = pending.pop(0)
            vc = vT_ref[:, cp * tk:(cp + 1) * tk]
            return acc1 + _dot(vc, p1), acc2 + _dot(vc, p2)

        for c in range(nk):
            side = jnp.where(c < cd, 1, jnp.where(c > cd, -1, 0))
            off = c_h * (side * (c * tk - j0)).astype(F32)
            p1 = jnp.exp2(slot.s[0, c * tk:(c + 1) * tk, :] - (m1 - off)).astype(BF16)
            p2 = jnp.exp2(slot.s[1, c * tk:(c + 1) * tk, :] - (m2 - off)).astype(BF16)
            pending.append((c, p1, p2))
            if len(pending) > PV_LAG:
                acc1, acc2 = value_matmul(acc1, acc2)
            if c < nk - 1:
                yield
        while pending:
            acc1, acc2 = value_matmul(acc1, acc2)
        lam = (jnp.exp(jnp.sum(lq1_ref[...] * lk1_ref[...], axis=1, keepdims=True))
               - jnp.exp(jnp.sum(lq2_ref[...] * lk2_ref[...], axis=1, keepdims=True))
               + LAMBDA_INIT)
        o1 = acc1[:HEAD_W] / acc1[HEAD_W:HEAD_W + 1]
        o2 = acc2[:HEAD_W] / acc2[HEAD_W:HEAD_W + 1]
        o = o1 - lam * o2
        on = o * lax.rsqrt(jnp.mean(o * o, axis=0, keepdims=True) + RMS_EPS)
        o_ref[i * tq:(i + 1) * tq, :] = (on.T * (gs_ref[...] * (1.0 - LAMBDA_INIT))).astype(BF16)
        yield

    sets = [[_Slot(scratch[p * per + i], scratch[(2 + p) * per + i]) for i in range(per)]
            for p in range(2)]
    n_groups = n_tiles // per

    def pass1_group(p, u):
        return itertools.chain(*[pass1_steps(sets[p][i], u * per + i, i) for i in range(per)])

    def pass2_group(p, u):
        return itertools.chain(*[pass2_steps(sets[p][i], u * per + i, i) for i in range(per)])

    inner = jnp.logical_and(step > 0, step < n_groups)

    @pl.when(step == 0)
    def _():
        for _ in pass1_group(0, step):
            pass

    for parity in range(2):
        @pl.when(jnp.logical_and(inner, (step & 1) == parity))
        def _():
            older = pass2_group(1 - parity, step - 1)
            newer = pass1_group(parity, step)
            for _ in range(PASS2_LEAD):
                next(older)
            for _ in range(per * nk - PASS2_LEAD):
                next(older)
                next(newer)
            for _ in range(PASS2_LEAD):
                next(newer)

    @pl.when(step == n_groups)
    def _():
        for _ in pass2_group((n_groups - 1) % 2, step - 1):
            pass


def _alibi_tables(tk, tq):
    c = (2.0 ** (-8.0 * np.arange(1, N_HEADS + 1) / N_HEADS) * math.log2(math.e)).astype(np.float32)

    def split3(x):
        x1 = x.astype(BF16)
        r = x - x1.astype(np.float32)
        x2 = r.astype(BF16)
        x3 = (r - x2.astype(np.float32)).astype(BF16)
        return x1, x2, x3

    augk = np.zeros((N_HEADS, 2, tk, HEAD_W), BF16)
    augq = np.zeros((N_HEADS, HEAD_W, tq), BF16)
    u = c[:, None] * np.arange(tk, dtype=np.float32)[None, :]
    v = c[:, None] * np.arange(tq, dtype=np.float32)[None, :]
    for n, (un, vn) in enumerate(zip(split3(u), split3(v))):
        augk[:, 0, :, n] = un
        augk[:, 0, :, 3 + n] = 1.0
        augq[:, n, :] = 1.0
        augq[:, 3 + n, :] = -vn
    augk[:, 1] = -augk[:, 0]
    pos = np.arange(tk // tq, dtype=np.float32)[:, None, None] * tq
    dist = np.abs(np.arange(tk, dtype=np.float32)[None, :, None]
                  - np.arange(tq, dtype=np.float32)[None, None, :] - pos)
    bias = (-c[:, None, None, None] * dist[None]).astype(np.float32)
    return c, augk.reshape(2 * N_HEADS, tk, HEAD_W), augq, bias


def _tail_kernel(x_ref, yf_ref, ya_ref, p_ref, gmix_ref, wbg_ref, bbg_ref, wao_ref, wo_ref,
                 gffn_ref, wg_ref, wu_ref, wd_ref, gple_ref, wpg_ref, wpp_ref, gfin_ref, out_ref):
    d = x_ref.shape[1]
    x = x_ref[...]
    h = _rms(x, gmix_ref[...]).astype(BF16)
    gates = jax.nn.sigmoid(_dot(h, wbg_ref[...]) + bbg_ref[...])
    ya = _dot(ya_ref[...], wao_ref[...])
    merged = gates[:, :d] * yf_ref[...] + gates[:, d:] * ya
    x = x + _dot(merged.astype(BF16), wo_ref[...])

    h2 = _rms(x, gffn_ref[...]).astype(BF16)
    gg = _dot(h2, wg_ref[...])
    uu = _dot(h2, wu_ref[...])
    act = (gg * jax.nn.sigmoid(gg)) * uu
    x = x + _dot(act.astype(BF16), wd_ref[...])

    h3 = _rms(x, gple_ref[...]).astype(BF16)
    pg = jax.nn.sigmoid(_dot(h3, wpg_ref[...]))
    x = x + pg * _dot(p_ref[...].astype(BF16), wpp_ref[...])
    out_ref[...] = _rms(x, gfin_ref[...])


def _const_spec(shape):
    return pl.BlockSpec(shape, lambda *_: (0,) * len(shape), pipeline_mode=pl.Buffered(1))


def _dft_tables(s_len):
    n1 = math.isqrt(s_len)
    n2 = s_len // n1
    assert n1 * n2 == s_len and n1 % 16 == 0 and n2 % 16 == 0

    def cos_sin(m, size):
        ang = 2.0 * np.pi * (m % size) / size
        return np.cos(ang), np.sin(ang)

    c1, s1 = cos_sin(np.outer(np.arange(n1), np.arange(n1)), n1)
    f1 = (np.concatenate([c1, -s1], axis=0) * n1 ** -0.5).astype(np.float32)
    kk = np.arange(n1)[:, None, None] + n1 * np.arange(n2)[None, :, None]
    c2, s2 = cos_sin(kk * np.arange(n2)[None, None, :], s_len)
    f2 = (np.concatenate([np.concatenate([c2, s2], axis=2), np.concatenate([-s2, c2], axis=2)],
                         axis=1) * n2 ** -0.5).astype(np.float32)
    cc, sc = cos_sin(np.outer(np.arange(GROUP_DIM), np.arange(GROUP_DIM)), GROUP_DIM)
    cs = (np.concatenate([cc, sc], axis=0) * GROUP_DIM ** -0.5).astype(np.float32)
    return n1, n2, f1, f2, cs


def kernel(x, p, g_mix, w_in, lambda_q1, lambda_k1, lambda_q2, lambda_k2, g_subln,
           w_fourier_out, w_attn_out, w_branch_gate, b_branch_gate, w_o, g_ffn, w_ffn_gate,
           w_ffn_up, w_ffn_down, g_ple, w_ple_gate, w_ple_proj, g_final):
    bsz, s_len, d = x.shape
    t_tok = bsz * s_len
    d_ff = w_ffn_gate.shape[-1]
    ple = p.shape[-1]
    assert w_in.shape[0] == 1 and d == N_HEADS * HEAD_W
    x2 = x.reshape(t_tok, d)
    p2 = p[0].reshape(t_tok, ple)
    n1, n2, f1, f2, cs = _dft_tables(s_len)
    cparams = functools.partial(pltpu.CompilerParams, vmem_limit_bytes=VMEM_LIMIT)

    tm = min(1024, s_len)
    row = lambda i: (i, 0)
    col = lambda i: (0, i)
    u_f, qT, k_arr, vT = pl.pallas_call(
        _in_proj_kernel,
        grid=(t_tok // tm,),
        in_specs=[pl.BlockSpec((tm, d), row), _const_spec((1, d)), _const_spec((d, 4 * d))],
        out_specs=[pl.BlockSpec((tm, d), row), pl.BlockSpec((d, tm), col),
                   pl.BlockSpec((tm, d), row), pl.BlockSpec((N_HEADS * V_ROWS, tm), col)],
        out_shape=[jax.ShapeDtypeStruct((t_tok, d), F32), jax.ShapeDtypeStruct((d, t_tok), BF16),
                   jax.ShapeDtypeStruct((t_tok, d), BF16),
                   jax.ShapeDtypeStruct((N_HEADS * V_ROWS, t_tok), BF16)],
        compiler_params=cparams(dimension_semantics=("arbitrary",)),
        name="in_proj",
    )(x2, g_mix[0][None, :], w_in[0].astype(BF16))

    r = min(FFT1_ROWS, n2)
    blk = pl.BlockSpec((n1, r, d), lambda b, j: (b, j, 0))
    t_re, t_im = pl.pallas_call(
        _fft1_kernel,
        grid=(bsz, n2 // r),
        in_specs=[blk, _const_spec((2 * n1, n1))],
        out_specs=[blk, blk],
        out_shape=[jax.ShapeDtypeStruct((bsz * n1, n2, d), BF16)] * 2,
        compiler_params=cparams(dimension_semantics=("arbitrary", "arbitrary")),
        name="fft_stage1",
    )(u_f.reshape(bsz * n1, n2, d), jnp.asarray(f1).astype(BF16))

    grp = min(FFT2_GROUP, n1)
    ngrp = n1 // grp
    tblk = pl.BlockSpec((grp * n2, d), row)
    y_f = pl.pallas_call(
        functools.partial(_fft2_kernel, n2=n2),
        grid=(bsz * ngrp,),
        in_specs=[tblk, tblk, pl.BlockSpec((grp, 2 * n2, 2 * n2), lambda s: (s % ngrp, 0, 0)),
                  _const_spec((2 * GROUP_DIM, GROUP_DIM)), _const_spec((d, d))],
        out_specs=pl.BlockSpec((n2, grp, d), lambda s: (s // ngrp, s % ngrp, 0)),
        out_shape=jax.ShapeDtypeStruct((bsz * n2, n1, d), F32),
        compiler_params=cparams(dimension_semantics=("arbitrary",)),
        name="fft_stage2",
    )(t_re.reshape(t_tok, d), t_im.reshape(t_tok, d), jnp.asarray(f2).astype(BF16),
      jnp.asarray(cs).astype(BF16),
      w_fourier_out[0].astype(BF16))
    y_f = y_f.reshape(t_tok, d)

    tq = min(256, s_len)
    tk = min(256, s_len)
    nq = s_len // tq
    cvec, augk, augq, bias = _alibi_tables(tk, tq)
    n_tiles = bsz * N_HEADS * nq

    per = TILES_PER_STEP
    npq = nq // per
    n_groups = n_tiles // per
    assert nq % per == 0

    def group(u):
        uh, tg = _divmod_pow2(u, npq)
        return _divmod_pow2(uh, N_HEADS) + (tg,)

    cur = lambda u: group(jnp.minimum(u, n_groups - 1))
    prev = lambda u: group(jnp.maximum(u - 1, 0))
    lam_spec = pl.BlockSpec((1, HEAD_DIM), lambda u: (0, 0))
    y_a = pl.pallas_call(
        functools.partial(_attn_kernel, tk=tk, tq=tq, nq=nq, n_tiles=n_tiles),
        grid=(n_groups + 1,),
        in_specs=[pl.BlockSpec(memory_space=pltpu.SMEM),
                  pl.BlockSpec((HEAD_W, per * tq), lambda u: (cur(u)[1], cur(u)[0] * npq + cur(u)[2])),
                  pl.BlockSpec((s_len, HEAD_W), lambda u: (cur(u)[0], cur(u)[1])),
                  pl.BlockSpec((V_ROWS, s_len), lambda u: (prev(u)[1], prev(u)[0])),
                  pl.BlockSpec((2, tk, HEAD_W), lambda u: (cur(u)[1], 0, 0)),
                  pl.BlockSpec((1, HEAD_W, tq), lambda u: (cur(u)[1], 0, 0)),
                  pl.BlockSpec((1, tk // tq, tk, tq), lambda u: (cur(u)[1], 0, 0, 0)),
                  lam_spec, lam_spec, lam_spec, lam_spec,
                  pl.BlockSpec((1, HEAD_W), lambda u: (0, 0))],
        out_specs=pl.BlockSpec((per * tq, HEAD_W),
                               lambda u: (prev(u)[0] * npq + prev(u)[2], prev(u)[1])),
        out_shape=jax.ShapeDtypeStruct((t_tok, d), BF16),
        scratch_shapes=([pltpu.VMEM((2, s_len, tq), F32)] * (2 * per)
                        + [pltpu.VMEM((2, 8, tq), F32)] * (2 * per)),
        compiler_params=cparams(dimension_semantics=("arbitrary",)),
        name="diff_attn",
    )(jnp.asarray(cvec), qT, k_arr, vT, jnp.asarray(augk), jnp.asarray(augq), jnp.asarray(bias),
      lambda_q1, lambda_k1, lambda_q2, lambda_k2, g_subln)

    tt = min(512, s_len)
    vec = lambda a: a.reshape(1, -1)
    out = pl.pallas_call(
        _tail_kernel,
        grid=(t_tok // tt,),
        in_specs=[pl.BlockSpec((tt, d), row), pl.BlockSpec((tt, d), row),
                  pl.BlockSpec((tt, d), row), pl.BlockSpec((tt, ple), row),
                  _const_spec((1, d)), _const_spec((d, 2 * d)), _const_spec((1, 2 * d)),
                  _const_spec((d, d)), _const_spec((d, d)),
                  _const_spec((1, d)), _const_spec((d, d_ff)), _const_spec((d, d_ff)),
                  _const_spec((d_ff, d)),
                  _const_spec((1, d)), _const_spec((d, d)), _const_spec((ple, d)),
                  _const_spec((1, d))],
        out_specs=pl.BlockSpec((tt, d), row),
        out_shape=jax.ShapeDtypeStruct((t_tok, d), F32),
        compiler_params=cparams(dimension_semantics=("arbitrary",)),
        name="tail",
    )(x2, y_f, y_a, p2, vec(g_mix[0]), w_branch_gate[0].astype(BF16), vec(b_branch_gate[0]),
      w_attn_out[0].astype(BF16), w_o[0].astype(BF16), vec(g_ffn[0]),
      w_ffn_gate[0].astype(BF16), w_ffn_up[0].astype(BF16), w_ffn_down[0].astype(BF16),
      vec(g_ple[0]), w_ple_gate[0].astype(BF16), w_ple_proj[0].astype(BF16), vec(g_final))
    return out.reshape(bsz, s_len, d)
```
